```python
import math
import jax, jax.numpy as jnp
from jax import lax
import numpy as np

D_MODEL = 2048
BATCH = 4
SEQ = 8192
DEPTH = 4

ATTN_HEADS = 8
QK_DIM = 64
V_DIM = 2 * QK_DIM
ATTN_WIDTH = ATTN_HEADS * V_DIM
Q_BLOCK = 128
ROPE_THETA = 10000.0
SSM_WIDTH = D_MODEL // 2
SSM_GROUP = 16
SSM_GROUPS = SSM_WIDTH // SSM_GROUP
SSM_STATE = 64
SSM_CHUNK = 128
D_FF = 4 * D_MODEL
LN_EPS = 1e-5
RMS_EPS = 1e-5
DEEPNORM_ALPHA = (2.0 * DEPTH) ** 0.25
DEEPNORM_BETA = (8.0 * DEPTH) ** -0.25

Q_COLS = ATTN_HEADS * 2 * QK_DIM
K_COLS = ATTN_HEADS * 2 * QK_DIM
V_COLS = ATTN_WIDTH
U_COLS = SSM_WIDTH
G_COLS = D_MODEL
OFF_K = Q_COLS
OFF_V = OFF_K + K_COLS
OFF_U = OFF_V + V_COLS
OFF_GA = OFF_U + U_COLS
OFF_GS = OFF_GA + G_COLS
IN_COLS = OFF_GS + G_COLS

kernel_name = "gated_diffattn_s5_hybrid_deepnorm"


def _lambda_init(layer):
    return 0.8 - 0.6 * math.exp(-0.3 * layer)


def layer_norm(x, g, b):
    xf = x.astype(jnp.float32)
    mu = jnp.mean(xf, axis=-1, keepdims=True)
    xc = xf - mu
    var = jnp.mean(jnp.square(xc), axis=-1, keepdims=True)
    y = xc * lax.rsqrt(var + LN_EPS) * g.astype(jnp.float32) + b.astype(jnp.float32)
    return y.astype(x.dtype)


def rope_tables(positions, dtype):
    inv = ROPE_THETA ** (-jnp.arange(0, QK_DIM, 2, dtype=jnp.float32) / QK_DIM)
    ang = positions.astype(jnp.float32)[..., None] * inv
    return jnp.cos(ang).astype(dtype), jnp.sin(ang).astype(dtype)


def apply_rope(t, cos, sin):
    c = cos[:, :, None, None, :]
    s = sin[:, :, None, None, :]
    half = QK_DIM // 2
    t1, t2 = t[..., :half], t[..., half:]
    return jnp.concatenate([t1 * c - t2 * s, t2 * c + t1 * s], axis=-1)


def diff_attention(q, k, v, lam, lam_init, subln_g):
    bsz, seq = q.shape[0], q.shape[1]
    n_blocks = seq // Q_BLOCK
    scale = QK_DIM ** -0.5
    kpos = jnp.arange(seq)

    def block(i):
        start = i * Q_BLOCK
        qb = lax.dynamic_slice_in_dim(q, start, Q_BLOCK, axis=1)
        s = jnp.einsum('bqhcd,bkhcd->bhcqk', qb, k).astype(jnp.float32) * scale
        qpos = start + jnp.arange(Q_BLOCK)
        mask = kpos[None, :] <= qpos[:, None]
        p = jax.nn.softmax(jnp.where(mask, s, -jnp.inf), axis=-1)
        w = p[:, :, 0] - lam * p[:, :, 1]
        return jnp.einsum('bhqk,bkhe->bqhe', w.astype(v.dtype), v)

    out = lax.map(block, jnp.arange(n_blocks))
    out = jnp.moveaxis(out, 0, 1).reshape(bsz, seq, ATTN_HEADS, V_DIM)
    of = out.astype(jnp.float32)
    of = of * lax.rsqrt(jnp.mean(jnp.square(of), axis=-1, keepdims=True) + RMS_EPS)
    of = of * subln_g.astype(jnp.float32) * (1.0 - lam_init)
    return of.astype(v.dtype).reshape(bsz, seq, ATTN_WIDTH)


def _complex_affine_combine(e1, e2):
    a1r, a1i, b1r, b1i = e1
    a2r, a2i, b2r, b2i = e2
    ar = a2r * a1r - a2i * a1i
    ai = a2r * a1i + a2i * a1r
    br = a2r * b1r - a2i * b1i + b2r
    bi = a2r * b1i + a2i * b1r + b2i
    return (ar, ai, br, bi)


def s5_ssm(u, a_re, a_im, log_dt, b_re, b_im, c_re, c_im, d_skip, w_glu):
    out_dtype = u.dtype
    f32 = jnp.float32
    uf = u.astype(f32)
    a_re = a_re.astype(f32); a_im = a_im.astype(f32)
    b_re = b_re.astype(f32); b_im = b_im.astype(f32)
    c_re = c_re.astype(f32); c_im = c_im.astype(f32)
    dt = jnp.exp(log_dt.astype(f32))[:, None]
    mag = jnp.exp(dt * a_re)
    ab_re = mag * jnp.cos(dt * a_im)
    ab_im = mag * jnp.sin(dt * a_im)
    den = jnp.square(a_re) + jnp.square(a_im)
    nr, ni = ab_re - 1.0, ab_im
    z_re = (nr * a_re + ni * a_im) / den
    z_im = (ni * a_re - nr * a_im) / den
    bb_re = z_re[..., None] * b_re - z_im[..., None] * b_im
    bb_im = z_re[..., None] * b_im + z_im[..., None] * b_re

    bsz, seq = uf.shape[0], uf.shape[1]
    n_chunks = seq // SSM_CHUNK
    uc = jnp.moveaxis(uf.reshape(bsz, n_chunks, SSM_CHUNK, SSM_GROUPS, SSM_GROUP), 1, 0)
    el_shape = (bsz, SSM_CHUNK, SSM_GROUPS, SSM_STATE)
    a_el_re = jnp.broadcast_to(ab_re, el_shape)
    a_el_im = jnp.broadcast_to(ab_im, el_shape)

    def step(carry, u_chunk):
        h_re, h_im = carry
        bu_re = jnp.einsum('blgh,gph->blgp', u_chunk, bb_re)
        bu_im = jnp.einsum('blgh,gph->blgp', u_chunk, bb_im)
        acum_re, acum_im, s_re, s_im = lax.associative_scan(
            _complex_affine_combine, (a_el_re, a_el_im, bu_re, bu_im), axis=1)
        x_re = s_re + acum_re * h_re[:, None] - acum_im * h_im[:, None]
        x_im = s_im + acum_re * h_im[:, None] + acum_im * h_re[:, None]
        y = (jnp.einsum('blgp,ghp->blgh', x_re, c_re)
             - jnp.einsum('blgp,ghp->blgh', x_im, c_im))
        return (x_re[:, -1], x_im[:, -1]), y

    init = (jnp.zeros((bsz, SSM_GROUPS, SSM_STATE), f32),
            jnp.zeros((bsz, SSM_GROUPS, SSM_STATE), f32))
    _, ys = lax.scan(step, init, uc)
    y = jnp.moveaxis(ys, 0, 1).reshape(bsz, seq, SSM_GROUPS, SSM_GROUP)
    y = y + d_skip.astype(f32) * uf
    g = jax.nn.gelu(y).reshape(bsz, seq, SSM_WIDTH)
    g = g * jax.nn.sigmoid(g @ w_glu.astype(f32))
    return g.astype(out_dtype)


def hybrid_mixer(x, cos, sin, w_in, lam_qk, lam_init, subln_g,
                 a_re, a_im, log_dt, b_re, b_im, c_re, c_im, d_skip, w_glu,
                 w_attn_up, w_ssm_up, w_out):
    bsz, seq, _ = x.shape
    q = (x @ w_in[:, :OFF_K]).reshape(bsz, seq, ATTN_HEADS, 2, QK_DIM)
    k = (x @ w_in[:, OFF_K:OFF_V]).reshape(bsz, seq, ATTN_HEADS, 2, QK_DIM)
    v = (x @ w_in[:, OFF_V:OFF_U]).reshape(bsz, seq, ATTN_HEADS, V_DIM)
    u = (x @ w_in[:, OFF_U:OFF_GA]).reshape(bsz, seq, SSM_GROUPS, SSM_GROUP)
    gate_a = jax.nn.sigmoid(x @ w_in[:, OFF_GA:OFF_GS])
    gate_s = jax.nn.sigmoid(x @ w_in[:, OFF_GS:])
    q = apply_rope(q, cos, sin)
    k = apply_rope(k, cos, sin)
    lq = lam_qk.astype(jnp.float32)
    lam = (jnp.exp(jnp.sum(lq[0] * lq[1])) - jnp.exp(jnp.sum(lq[2] * lq[3])) + lam_init)
    attn = diff_attention(q, k, v, lam, lam_init, subln_g)
    ssm = s5_ssm(u, a_re, a_im, log_dt, b_re, b_im, c_re, c_im, d_skip, w_glu)
    merged = gate_a * (attn @ w_attn_up) + gate_s * (ssm @ w_ssm_up)
    return merged @ w_out


def setup_inputs(seed: int = 0) -> dict:
    key = jax.random.key(seed)
    ks = jax.random.split(key, 24)
    f32 = jnp.float32
    nrm = lambda k, shape, std: jax.random.normal(k, shape, f32) * std
    x = jax.random.normal(ks[0], (BATCH, SEQ, D_MODEL), f32)
    positions = jnp.broadcast_to(jnp.arange(SEQ, dtype=jnp.int32)[None, :], (BATCH, SEQ))
    w_in = nrm(ks[1], (DEPTH, D_MODEL, IN_COLS), D_MODEL ** -0.5)
    lambda_qk = nrm(ks[2], (DEPTH, 4, QK_DIM), 0.1)
    subln_g = 1.0 + nrm(ks[3], (DEPTH, V_DIM), 0.02)
    ssm_a_re = -0.5 + nrm(ks[4], (DEPTH, SSM_GROUPS, SSM_STATE), 0.01)
    ssm_a_im = jnp.broadcast_to(jnp.pi * jnp.arange(SSM_STATE, dtype=f32), (DEPTH, SSM_GROUPS, SSM_STATE))
    ssm_log_dt = jax.random.uniform(ks[5], (DEPTH, SSM_GROUPS), f32, math.log(1e-3), math.log(1e-1))
    ssm_b_re = nrm(ks[6], (DEPTH, SSM_GROUPS, SSM_STATE, SSM_GROUP), (2.0 * SSM_GROUP) ** -0.5)
    ssm_b_im = nrm(ks[7], (DEPTH, SSM_GROUPS, SSM_STATE, SSM_GROUP), (2.0 * SSM_GROUP) ** -0.5)
    ssm_c_re = nrm(ks[8], (DEPTH, SSM_GROUPS, SSM_GROUP, SSM_STATE), 0.5 ** 0.5)
    ssm_c_im = nrm(ks[9], (DEPTH, SSM_GROUPS, SSM_GROUP, SSM_STATE), 0.5 ** 0.5)
    ssm_d = nrm(ks[10], (DEPTH, SSM_GROUPS, SSM_GROUP), 1.0)
    w_glu = nrm(ks[11], (DEPTH, SSM_WIDTH, SSM_WIDTH), SSM_WIDTH ** -0.5)
    w_attn_up = nrm(ks[12], (DEPTH, ATTN_WIDTH, D_MODEL), ATTN_WIDTH ** -0.5)
    w_ssm_up = nrm(ks[13], (DEPTH, SSM_WIDTH, D_MODEL), SSM_WIDTH ** -0.5)
    w_out = nrm(ks[14], (DEPTH, D_MODEL, D_MODEL), D_MODEL ** -0.5 * DEEPNORM_BETA)
    ln1_g = 1.0 + nrm(ks[15], (DEPTH, D_MODEL), 0.02)
    ln1_b = nrm(ks[16], (DEPTH, D_MODEL), 0.02)
    ln2_g = 1.0 + nrm(ks[17], (DEPTH, D_MODEL), 0.02)
    ln2_b = nrm(ks[18], (DEPTH, D_MODEL), 0.02)
    w_mlp_up = nrm(ks[19], (DEPTH, D_MODEL, D_FF), D_MODEL ** -0.5)
    w_mlp_down = nrm(ks[20], (DEPTH, D_FF, D_MODEL), D_FF ** -0.5 * DEEPNORM_BETA)
    return {"x": x, "positions": positions, "w_in": w_in, "lambda_qk": lambda_qk,
            "subln_g": subln_g, "ssm_a_re": ssm_a_re, "ssm_a_im": ssm_a_im,
            "ssm_log_dt": ssm_log_dt, "ssm_b_re": ssm_b_re, "ssm_b_im": ssm_b_im,
            "ssm_c_re": ssm_c_re, "ssm_c_im": ssm_c_im, "ssm_d": ssm_d, "w_glu": w_glu,
            "w_attn_up": w_attn_up, "w_ssm_up": w_ssm_up, "w_out": w_out,
            "ln1_g": ln1_g, "ln1_b": ln1_b, "ln2_g": ln2_g, "ln2_b": ln2_b,
            "w_mlp_up": w_mlp_up, "w_mlp_down": w_mlp_down}


def reference(x, positions, w_in, lambda_qk, subln_g, ssm_a_re, ssm_a_im, ssm_log_dt,
              ssm_b_re, ssm_b_im, ssm_c_re, ssm_c_im, ssm_d, w_glu, w_attn_up, w_ssm_up,
              w_out, ln1_g, ln1_b, ln2_g, ln2_b, w_mlp_up, w_mlp_down):
    cos, sin = rope_tables(positions, x.dtype)
    for l in range(DEPTH):
        lam_init = _lambda_init(l)
        mix = hybrid_mixer(x, cos, sin, w_in[l], lambda_qk[l], lam_init, subln_g[l],
                           ssm_a_re[l], ssm_a_im[l], ssm_log_dt[l], ssm_b_re[l], ssm_b_im[l],
                           ssm_c_re[l], ssm_c_im[l], ssm_d[l], w_glu[l],
                           w_attn_up[l], w_ssm_up[l], w_out[l])
        x = layer_norm(DEEPNORM_ALPHA * x + mix, ln1_g[l], ln1_b[l])
        h = jnp.square(jax.nn.relu(x @ w_mlp_up[l])) @ w_mlp_down[l]
        x = layer_norm(DEEPNORM_ALPHA * x + h, ln2_g[l], ln2_b[l])
    return x
```

```python
import functools
import math

import jax
import jax.numpy as jnp
from jax import lax
from jax.experimental import pallas as pl
from jax.experimental.pallas import tpu as pltpu

F32 = jnp.float32
BF16 = jnp.bfloat16

QK_DIM = 64
V_DIM = 2 * QK_DIM
HEAD_COLS = 2 * QK_DIM
SSM_GROUP = 16
SSM_STATE = 64
ROPE_THETA = 10000.0
LN_EPS = 1e-5
RMS_EPS = 1e-5
LANES = 128
SUBLANES = 8
VMEM_LIMIT = 56 * 1024 * 1024
NEG_BIG = -1e30


def _lambda_init(layer):
    return 0.8 - 0.6 * math.exp(-0.3 * layer)


def _params(*sem):
    return pltpu.CompilerParams(dimension_semantics=sem, vmem_limit_bytes=VMEM_LIMIT)


def _in_proj_kernel(x_ref, w_ref, cos_ref, sin_ref, o_ref, *, q_tiles, rope_tiles, plain_tiles):
    j = pl.program_id(1)
    acc = jnp.dot(x_ref[...].astype(BF16), w_ref[...], preferred_element_type=F32)
    tn = acc.shape[1]

    @pl.when(j < rope_tiles)
    def _():
        reps = tn // LANES
        c = jnp.tile(cos_ref[...], (1, reps))
        s = jnp.tile(sin_ref[...], (1, reps))
        lane = lax.broadcasted_iota(jnp.int32, acc.shape, 1)
        first_half = (lane % QK_DIM) < (QK_DIM // 2)
        rot = jnp.where(first_half, pltpu.roll(acc, tn - QK_DIM // 2, 1),
                        pltpu.roll(acc, QK_DIM // 2, 1))
        scale = jnp.where(j < q_tiles, QK_DIM ** -0.5, 1.0).astype(F32)
        o_ref[...] = ((acc * c + rot * s) * scale).astype(o_ref.dtype)

    @pl.when((j >= rope_tiles) & (j < rope_tiles + plain_tiles))
    def _():
        o_ref[...] = acc.astype(o_ref.dtype)

    @pl.when(j >= rope_tiles + plain_tiles)
    def _():
        o_ref[...] = jax.nn.sigmoid(acc).astype(o_ref.dtype)


def _in_proj(x2, w_bf, cos_t, sin_t, *, q_cols, rope_cols, plain_cols):
    n, d = x2.shape
    cols = w_bf.shape[1]
    tm = min(512, n)
    tn = min(1024, q_cols)
    kern = functools.partial(_in_proj_kernel, q_tiles=q_cols // tn, rope_tiles=rope_cols // tn,
                             plain_tiles=plain_cols // tn)
    return pl.pallas_call(
        kern,
        grid=(n // tm, cols // tn),
        in_specs=[
            pl.BlockSpec((tm, d), lambda i, j: (i, 0)),
            pl.BlockSpec((d, tn), lambda i, j: (0, j)),
            pl.BlockSpec((tm, LANES), lambda i, j: (i, 0)),
            pl.BlockSpec((tm, LANES), lambda i, j: (i, 0)),
        ],
        out_specs=pl.BlockSpec((tm, tn), lambda i, j: (i, j)),
        out_shape=jax.ShapeDtypeStruct((n, cols), BF16),
        compiler_params=_params("parallel", "arbitrary"),
        name="in_proj",
    )(x2, w_bf, cos_t, sin_t)


def _attn_kernel(lq_ref, g_ref, q_ref, k_ref, v_ref, o_ref,
                 m1, l1, a1, m2, l2, a2, *, lam_init):
    qi = pl.program_id(2)
    ki = pl.program_id(3)
    tq = q_ref.shape[0]
    tk = k_ref.shape[0]

    @pl.when(ki == 0)
    def _():
        for m, l, a in ((m1, l1, a1), (m2, l2, a2)):
            m[...] = jnp.full(m.shape, NEG_BIG, F32)
            l[...] = jnp.zeros(l.shape, F32)
            a[...] = jnp.zeros(a.shape, F32)

    def step(masked):
        q = q_ref[...]
        k = k_ref[...]
        v = v_ref[...]
        if masked:
            row = lax.broadcasted_iota(jnp.int32, (tq, tk), 0)
            col = lax.broadcasted_iota(jnp.int32, (tq, tk), 1)
            keep = col <= row
        for c, (m, l, a) in enumerate(((m1, l1, a1), (m2, l2, a2))):
            qc = q[:, c * QK_DIM:(c + 1) * QK_DIM]
            kc = k[:, c * QK_DIM:(c + 1) * QK_DIM]
            s = lax.dot_general(qc, kc, (((1,), (1,)), ((), ())), preferred_element_type=F32)
            if masked:
                s = jnp.where(keep, s, NEG_BIG)
            m_prev = m[...]
            m_new = jnp.maximum(m_prev, jnp.max(s, axis=1, keepdims=True))
            alpha = jnp.exp(m_prev - m_new)
            p = jnp.exp(s - m_new)
            l[...] = alpha * l[...] + jnp.sum(p, axis=1, keepdims=True)
            a[...] = alpha * a[...] + jnp.dot(p.astype(BF16), v, preferred_element_type=F32)
            m[...] = m_new

    @pl.when(ki < qi)
    def _():
        step(False)

    @pl.when(ki == qi)
    def _():
        step(True)
        lq = lq_ref[...]
        lam = (jnp.exp(jnp.sum(lq[0:1] * lq[1:2], axis=1, keepdims=True))
               - jnp.exp(jnp.sum(lq[2:3] * lq[3:4], axis=1, keepdims=True)) + lam_init)
        out = a1[...] / l1[...] - lam * (a2[...] / l2[...])
        out = out * lax.rsqrt(jnp.mean(out * out, axis=1, keepdims=True) + RMS_EPS)
        out = out * g_ref[...] * (1.0 - lam_init)
        o_ref[...] = out.astype(o_ref.dtype)


def _diff_attn(proj, lam_qk, subln_g, *, bsz, seq, heads, lam_init):
    n = proj.shape[0]
    t = min(512, seq)
    nq = seq // t
    kern = functools.partial(_attn_kernel, lam_init=lam_init)
    return pl.pallas_call(
        kern,
        grid=(bsz, heads, nq, nq),
        in_specs=[
            pl.BlockSpec((4, QK_DIM), lambda b, h, qi, ki: (0, 0)),
            pl.BlockSpec((1, V_DIM), lambda b, h, qi, ki: (0, 0)),
            pl.BlockSpec((t, HEAD_COLS), lambda b, h, qi, ki: (b * nq + qi, h)),
            pl.BlockSpec((t, HEAD_COLS), lambda b, h, qi, ki: (b * nq + jnp.minimum(ki, qi), heads + h)),
            pl.BlockSpec((t, V_DIM), lambda b, h, qi, ki: (b * nq + jnp.minimum(ki, qi), 2 * heads + h)),
        ],
        out_specs=pl.BlockSpec((t, V_DIM), lambda b, h, qi, ki: (b * nq + qi, h)),
        out_shape=jax.ShapeDtypeStruct((n, heads * V_DIM), BF16),
        scratch_shapes=[
            pltpu.VMEM((t, 1), F32), pltpu.VMEM((t, 1), F32), pltpu.VMEM((t, V_DIM), F32),
            pltpu.VMEM((t, 1), F32), pltpu.VMEM((t, 1), F32), pltpu.VMEM((t, V_DIM), F32),
        ],
        compiler_params=_params("parallel", "parallel", "parallel", "arbitrary"),
        name="diff_attn",
    )(lam_qk, subln_g.reshape(1, V_DIM), proj, proj, proj)


def _cmul(ar, ai, br, bi):
    return ar * br - ai * bi, ar * bi + ai * br


def _ssm_prep_kernel(are_ref, aim_ref, ldt_ref, bre_ref, bim_ref,
                     bbre_ref, bbim_ref, pre_ref, pim_ref, hre_ref, him_ref):
    a_re = are_ref[...]
    a_im = aim_ref[...]
    dt = jnp.exp(ldt_ref[...])
    mag = jnp.exp(dt * a_re)
    ab_re = mag * jnp.cos(dt * a_im)
    ab_im = mag * jnp.sin(dt * a_im)
    den = a_re * a_re + a_im * a_im
    nr, ni = ab_re - 1.0, ab_im
    z_re = (nr * a_re + ni * a_im) / den
    z_im = (ni * a_re - nr * a_im) / den
    b_re = bre_ref[...]
    b_im = bim_ref[...]
    bbre_ref[...] = z_re * b_re - z_im * b_im
    bbim_ref[...] = z_re * b_im + z_im * b_re
    sub = pre_ref.shape[0]
    pre_ref[0:1, :] = ab_re
    pim_ref[0:1, :] = ab_im
    filled = 1
    while filled < sub:
        tr = pre_ref[filled - 1:filled, :]
        ti = pim_ref[filled - 1:filled, :]
        nr2, ni2 = _cmul(pre_ref[0:filled, :], pim_ref[0:filled, :], tr, ti)
        pre_ref[filled:2 * filled, :] = nr2
        pim_ref[filled:2 * filled, :] = ni2
        filled *= 2
    er = pre_ref[sub - 1:sub, :]
    ei = pim_ref[sub - 1:sub, :]
    for r in range(3):
        hre_ref[r:r + 1, :] = er
        him_ref[r:r + 1, :] = ei
        er, ei = _cmul(er, ei, er, ei)


def _ssm_prep(a_re, a_im, log_dt, b_re, b_im, *, sub):
    g, p = a_re.shape
    w = g * p
    row = lambda t: t.reshape(1, w)
    ldt = jnp.broadcast_to(log_dt[:, None], (g, p))
    tb = lambda t: jnp.transpose(t, (2, 0, 1)).reshape(SSM_GROUP, w)
    full = lambda r: pl.BlockSpec((r, w), lambda: (0, 0))
    shapes = [(SSM_GROUP, w), (SSM_GROUP, w), (sub, w), (sub, w), (3, w), (3, w)]
    return pl.pallas_call(
        _ssm_prep_kernel,
        in_specs=[full(1), full(1), full(1), full(SSM_GROUP), full(SSM_GROUP)],
        out_specs=[full(s[0]) for s in shapes],
        out_shape=[jax.ShapeDtypeStruct(s, F32) for s in shapes],
        compiler_params=pltpu.CompilerParams(vmem_limit_bytes=VMEM_LIMIT),
        name="ssm_prep",
    )(row(a_re), row(a_im), row(ldt), tb(b_re), tb(b_im))


def _block_diag_tiles(blocks, per_tile):
    g, r, c = blocks.shape
    t = blocks.reshape(g // per_tile, per_tile, r, c)
    eye = jnp.eye(per_tile, dtype=blocks.dtype)
    out = t[:, :, :, None, :] * eye[None, :, None, :, None]
    return out.reshape(g // per_tile, per_tile * r, per_tile * c)


GROUPS_PER_TILE = LANES // SSM_GROUP
STATES_PER_TILE = GROUPS_PER_TILE * SSM_STATE
SCAN_STRIP = 512


def _ssm_kernel(u_ref, perm_ref, permt_ref, bb_ref, cre_ref, cim_ref, d_ref, wglu_ref,
                are_ref, aim_ref, pre_ref, pim_ref, hre_ref, him_ref,
                o_ref, xr_ref, xi_ref, hr_ref, hi_ref):
    ci = pl.program_id(1)
    chunk = u_ref.shape[0]
    sub = chunk // SUBLANES
    width = xr_ref.shape[1]
    n_tiles = width // STATES_PER_TILE

    @pl.when(ci == 0)
    def _():
        hr_ref[...] = jnp.zeros(hr_ref.shape, F32)
        hi_ref[...] = jnp.zeros(hi_ref.shape, F32)

    u = jnp.dot(perm_ref[...], u_ref[...], preferred_element_type=F32)
    ub = u.astype(BF16)

    for t in range(n_tiles):
        bu = jnp.dot(ub[:, t * LANES:(t + 1) * LANES], bb_ref[t], preferred_element_type=F32)
        sl = slice(t * STATES_PER_TILE, (t + 1) * STATES_PER_TILE)
        xr_ref[:, sl] = bu[:, :STATES_PER_TILE]
        xi_ref[:, sl] = bu[:, STATES_PER_TILE:]

    row8 = lax.broadcasted_iota(jnp.int32, (SUBLANES, SCAN_STRIP), 0)
    for st in range(width // SCAN_STRIP):
        sl = slice(st * SCAN_STRIP, (st + 1) * SCAN_STRIP)
        ar = jnp.broadcast_to(are_ref[:, sl], (SUBLANES, SCAN_STRIP))
        ai = jnp.broadcast_to(aim_ref[:, sl], (SUBLANES, SCAN_STRIP))
        x0r = jnp.where(row8 == 0, jnp.broadcast_to(hr_ref[:, sl], (SUBLANES, SCAN_STRIP)), 0.0)
        x0i = jnp.where(row8 == 0, jnp.broadcast_to(hi_ref[:, sl], (SUBLANES, SCAN_STRIP)), 0.0)

        def local_scan(j, carry):
            xr, xi = carry
            r0 = pl.multiple_of(j * SUBLANES, SUBLANES)
            nr = ar * xr - ai * xi + xr_ref[pl.ds(r0, SUBLANES), sl]
            ni = ar * xi + ai * xr + xi_ref[pl.ds(r0, SUBLANES), sl]
            xr_ref[pl.ds(r0, SUBLANES), sl] = nr
            xi_ref[pl.ds(r0, SUBLANES), sl] = ni
            return nr, ni

        er, ei = lax.fori_loop(0, sub, local_scan, (x0r, x0i), unroll=4)

        for r, d in enumerate((1, 2, 4)):
            pr = jnp.broadcast_to(hre_ref[r:r + 1, sl], (SUBLANES, SCAN_STRIP))
            pi = jnp.broadcast_to(him_ref[r:r + 1, sl], (SUBLANES, SCAN_STRIP))
            sr = jnp.where(row8 >= d, pltpu.roll(er, d, 0), 0.0)
            si = jnp.where(row8 >= d, pltpu.roll(ei, d, 0), 0.0)
            er, ei = er + pr * sr - pi * si, ei + pr * si + pi * sr
        hr_ref[:, sl] = er[SUBLANES - 1:SUBLANES, :]
        hi_ref[:, sl] = ei[SUBLANES - 1:SUBLANES, :]
        cr = jnp.where(row8 >= 1, pltpu.roll(er, 1, 0), 0.0)
        cim = jnp.where(row8 >= 1, pltpu.roll(ei, 1, 0), 0.0)

        def fixup(j, carry):
            r0 = pl.multiple_of(j * SUBLANES, SUBLANES)
            pr = jnp.broadcast_to(pre_ref[pl.ds(j, 1), sl], (SUBLANES, SCAN_STRIP))
            pi = jnp.broadcast_to(pim_ref[pl.ds(j, 1), sl], (SUBLANES, SCAN_STRIP))
            nr = xr_ref[pl.ds(r0, SUBLANES), sl] + pr * cr - pi * cim
            ni = xi_ref[pl.ds(r0, SUBLANES), sl] + pr * cim + pi * cr
            xr_ref[pl.ds(r0, SUBLANES), sl] = nr
            xi_ref[pl.ds(r0, SUBLANES), sl] = ni
            return carry

        lax.fori_loop(0, sub, fixup, 0, unroll=4)

    ys = []
    for t in range(n_tiles):
        sl = slice(t * STATES_PER_TILE, (t + 1) * STATES_PER_TILE)
        ys.append(jnp.dot(xr_ref[:, sl].astype(BF16), cre_ref[t], preferred_element_type=F32)
                  - jnp.dot(xi_ref[:, sl].astype(BF16), cim_ref[t], preferred_element_type=F32))
    y = jnp.concatenate(ys, axis=1) + d_ref[...] * u
    g = jax.nn.gelu(y)
    z = jnp.dot(g.astype(BF16), wglu_ref[...], preferred_element_type=F32)
    out = (g * jax.nn.sigmoid(z)).astype(BF16)
    o_ref[...] = jnp.dot(permt_ref[...], out, preferred_element_type=F32).astype(o_ref.dtype)


def _s5_ssm(proj, u_col_block, prep, c_re, c_im, d_skip, w_glu_bf, *, bsz, seq, chunk):
    bb_re, bb_im, pow_re, pow_im, hs_re, hs_im = prep
    ab_re, ab_im = pow_re[0:1], pow_im[0:1]
    n = proj.shape[0]
    groups = c_re.shape[0]
    ssm_w = groups * SSM_GROUP
    width = groups * SSM_STATE
    n_tiles = ssm_w // LANES
    sub = chunk // SUBLANES
    nc = seq // chunk
    to_blocks = lambda t: jnp.transpose(t.reshape(SSM_GROUP, groups, SSM_STATE), (1, 0, 2))
    bb = jnp.concatenate([_block_diag_tiles(to_blocks(bb_re), GROUPS_PER_TILE),
                          _block_diag_tiles(to_blocks(bb_im), GROUPS_PER_TILE)], axis=2).astype(BF16)
    ct = lambda t: _block_diag_tiles(jnp.transpose(t, (0, 2, 1)), GROUPS_PER_TILE).astype(BF16)
    r = jnp.arange(chunk)
    src = (r % SUBLANES) * sub + r // SUBLANES
    perm = (src[:, None] == jnp.arange(chunk)[None, :]).astype(BF16)
    const2 = lambda shape: pl.BlockSpec(shape, lambda b, c: (0, 0))
    const3 = lambda shape: pl.BlockSpec(shape, lambda b, c: (0, 0, 0))
    return pl.pallas_call(
        _ssm_kernel,
        grid=(bsz, nc),
        in_specs=[
            pl.BlockSpec((chunk, ssm_w), lambda b, c: (b * nc + c, u_col_block)),
            const2((chunk, chunk)), const2((chunk, chunk)),
            const3((n_tiles, LANES, 2 * STATES_PER_TILE)),
            const3((n_tiles, STATES_PER_TILE, LANES)), const3((n_tiles, STATES_PER_TILE, LANES)),
            const2((1, ssm_w)), const2((ssm_w, ssm_w)),
            const2((1, width)), const2((1, width)),
            const2((sub, width)), const2((sub, width)),
            const2((3, width)), const2((3, width)),
        ],
        out_specs=pl.BlockSpec((chunk, ssm_w), lambda b, c: (b * nc + c, 0)),
        out_shape=jax.ShapeDtypeStruct((n, ssm_w), BF16),
        scratch_shapes=[
            pltpu.VMEM((chunk, width), F32), pltpu.VMEM((chunk, width), F32),
            pltpu.VMEM((1, width), F32), pltpu.VMEM((1, width), F32),
        ],
        compiler_params=_params("parallel", "arbitrary"),
        name="s5_ssm",
    )(proj, perm, perm.T, bb, ct(c_re), ct(c_im), d_skip.reshape(1, ssm_w), w_glu_bf,
      ab_re, ab_im, pow_re, pow_im, hs_re, hs_im)


def _layer_norm(y, g, b):
    mu = jnp.mean(y, axis=1, keepdims=True)
    yc = y - mu
    var = jnp.mean(yc * yc, axis=1, keepdims=True)
    return yc * lax.rsqrt(var + LN_EPS) * g + b


def _mix_kernel(x_ref, attn_ref, ssm_ref, ga_ref, gs_ref, wa_ref, ws_ref, wo_ref, g_ref, b_ref,
                o_ref, *, alpha):
    a = jnp.dot(attn_ref[...], wa_ref[...], preferred_element_type=F32)
    s = jnp.dot(ssm_ref[...], ws_ref[...], preferred_element_type=F32)
    merged = ga_ref[...].astype(F32) * a + gs_ref[...].astype(F32) * s
    mix = jnp.dot(merged.astype(BF16), wo_ref[...], preferred_element_type=F32)
    o_ref[...] = _layer_norm(alpha * x_ref[...] + mix, g_ref[...], b_ref[...])


def _mix_out(x2, attn, ssm, proj, ga_block, wa, ws, wo, ln_g, ln_b, *, alpha):
    n, d = x2.shape
    aw = attn.shape[1]
    sw = ssm.shape[1]
    tm = min(256, n)
    row = lambda w: pl.BlockSpec((tm, w), lambda i: (i, 0))
    const = lambda shape: pl.BlockSpec(shape, lambda i: (0, 0), pipeline_mode=pl.Buffered(1))
    return pl.pallas_call(
        functools.partial(_mix_kernel, alpha=alpha),
        grid=(n // tm,),
        in_specs=[
            row(d), row(aw), row(sw),
            pl.BlockSpec((tm, d), lambda i: (i, ga_block)),
            pl.BlockSpec((tm, d), lambda i: (i, ga_block + 1)),
            const((aw, d)), const((sw, d)), const((d, d)), const((1, d)), const((1, d)),
        ],
        out_specs=row(d),
        out_shape=jax.ShapeDtypeStruct((n, d), F32),
        compiler_params=_params("parallel"),
        name="mix_out",
    )(x2, attn, ssm, proj, proj, wa, ws, wo, ln_g.reshape(1, d), ln_b.reshape(1, d))


def _mlp_kernel(x_ref, w1_ref, w2_ref, g_ref, b_ref, o_ref, xb_ref, acc_ref, *, alpha):
    f = pl.program_id(1)

    @pl.when(f == 0)
    def _():
        xb_ref[...] = x_ref[...].astype(BF16)
        acc_ref[...] = jnp.zeros(acc_ref.shape, F32)

    h = jnp.dot(xb_ref[...], w1_ref[...], preferred_element_type=F32)
    h = jnp.square(jnp.maximum(h, 0.0))
    acc_ref[...] += jnp.dot(h.astype(BF16), w2_ref[...], preferred_element_type=F32)

    @pl.when(f == pl.num_programs(1) - 1)
    def _():
        o_ref[...] = _layer_norm(alpha * x_ref[...] + acc_ref[...], g_ref[...], b_ref[...])


def _mlp(x2, w1, w2, ln_g, ln_b, *, alpha):
    n, d = x2.shape
    dff = w1.shape[1]
    tm = min(512, n)
    tf = min(512, dff)
    return pl.pallas_call(
        functools.partial(_mlp_kernel, alpha=alpha),
        grid=(n // tm, dff // tf),
        in_specs=[
            pl.BlockSpec((tm, d), lambda i, f: (i, 0)),
            pl.BlockSpec((d, tf), lambda i, f: (0, f)),
            pl.BlockSpec((tf, d), lambda i, f: (f, 0)),
            pl.BlockSpec((1, d), lambda i, f: (0, 0)),
            pl.BlockSpec((1, d), lambda i, f: (0, 0)),
        ],
        out_specs=pl.BlockSpec((tm, d), lambda i, f: (i, 0)),
        out_shape=jax.ShapeDtypeStruct((n, d), F32),
        scratch_shapes=[pltpu.VMEM((tm, d), BF16), pltpu.VMEM((tm, d), F32)],
        compiler_params=_params("parallel", "arbitrary"),
        name="mlp",
    )(x2, w1, w2, ln_g.reshape(1, d), ln_b.reshape(1, d))


def _rope_tables(positions):
    inv = ROPE_THETA ** (-jnp.arange(0, QK_DIM, 2, dtype=F32) / QK_DIM)
    ang = positions.astype(F32).reshape(-1, 1) * inv
    cos = jnp.cos(ang)
    sin = jnp.sin(ang)
    reps = LANES // QK_DIM
    return (jnp.tile(jnp.concatenate([cos, cos], axis=1), (1, reps)),
            jnp.tile(jnp.concatenate([-sin, sin], axis=1), (1, reps)))


def kernel(x, positions, w_in, lambda_qk, subln_g, ssm_a_re, ssm_a_im, ssm_log_dt, ssm_b_re,
           ssm_b_im, ssm_c_re, ssm_c_im, ssm_d, w_glu, w_attn_up, w_ssm_up, w_out, ln1_g, ln1_b,
           ln2_g, ln2_b, w_mlp_up, w_mlp_down):
    bsz, seq, d = x.shape
    depth = w_in.shape[0]
    attn_w = w_attn_up.shape[1]
    ssm_w = w_ssm_up.shape[1]
    heads = attn_w // V_DIM
    q_cols = heads * HEAD_COLS
    assert w_in.shape[2] == 2 * q_cols + attn_w + ssm_w + 2 * d
    assert q_cols == attn_w == ssm_w and d % ssm_w == 0
    alpha = (2.0 * depth) ** 0.25
    chunk = min(512, seq)

    cos_t, sin_t = _rope_tables(positions)
    x2 = x.reshape(bsz * seq, d)
    for l in range(depth):
        lam_init = _lambda_init(l)
        proj = _in_proj(x2, w_in[l].astype(BF16), cos_t, sin_t, q_cols=q_cols,
                        rope_cols=2 * q_cols, plain_cols=attn_w + ssm_w)
        attn = _diff_attn(proj, lambda_qk[l], subln_g[l], bsz=bsz, seq=seq, heads=heads,
                          lam_init=lam_init)
        prep = _ssm_prep(ssm_a_re[l], ssm_a_im[l], ssm_log_dt[l], ssm_b_re[l], ssm_b_im[l],
                         sub=chunk // SUBLANES)
        ssm = _s5_ssm(proj, (2 * q_cols + attn_w) // ssm_w, prep, ssm_c_re[l], ssm_c_im[l],
                      ssm_d[l], w_glu[l].astype(BF16), bsz=bsz, seq=seq, chunk=chunk)
        x2 = _mix_out(x2, attn, ssm, proj, (2 * q_cols + attn_w + ssm_w) // d,
                      w_attn_up[l].astype(BF16), w_ssm_up[l].astype(BF16), w_out[l].astype(BF16),
                      ln1_g[l], ln1_b[l], alpha=alpha)
        x2 = _mlp(x2, w_mlp_up[l].astype(BF16), w_mlp_down[l].astype(BF16), ln2_g[l], ln2_b[l],
                  alpha=alpha)
    return x2.reshape(bsz, seq, d)
```

```python
import functools
import math

import jax
import jax.numpy as jnp
from jax import lax
from jax.experimental import pallas as pl
from jax.experimental.pallas import tpu as pltpu

F32 = jnp.float32
BF16 = jnp.bfloat16

QK_DIM = 64
V_DIM = 2 * QK_DIM
HEAD_COLS = 2 * QK_DIM
SSM_GROUP = 16
SSM_STATE = 64
ROPE_THETA = 10000.0
LN_EPS = 1e-5
RMS_EPS = 1e-5
LANES = 128
SUBLANES = 8
VMEM_LIMIT = 56 * 1024 * 1024
NEG_BIG = -1e30


def _lambda_init(layer):
    return 0.8 - 0.6 * math.exp(-0.3 * layer)


def _params(*sem):
    return pltpu.CompilerParams(dimension_semantics=sem, vmem_limit_bytes=VMEM_LIMIT)


def _in_proj_kernel(x_ref, w_ref, cos_ref, sin_ref, o_ref, *, q_tiles, rope_tiles, plain_tiles):
    j = pl.program_id(1)
    acc = jnp.dot(x_ref[...].astype(BF16), w_ref[...], preferred_element_type=F32)
    tn = acc.shape[1]

    @pl.when(j < rope_tiles)
    def _():
        reps = tn // LANES
        c = jnp.tile(cos_ref[...], (1, reps))
        s = jnp.tile(sin_ref[...], (1, reps))
        lane = lax.broadcasted_iota(jnp.int32, acc.shape, 1)
        first_half = (lane % QK_DIM) < (QK_DIM // 2)
        rot = jnp.where(first_half, pltpu.roll(acc, tn - QK_DIM // 2, 1),
                        pltpu.roll(acc, QK_DIM // 2, 1))
        scale = jnp.where(j < q_tiles, QK_DIM ** -0.5 * math.log2(math.e), 1.0).astype(F32)
        o_ref[...] = ((acc * c + rot * s) * scale).astype(o_ref.dtype)

    @pl.when((j >= rope_tiles) & (j < rope_tiles + plain_tiles))
    def _():
        o_ref[...] = acc.astype(o_ref.dtype)

    @pl.when(j >= rope_tiles + plain_tiles)
    def _():
        o_ref[...] = jax.nn.sigmoid(acc).astype(o_ref.dtype)


def _in_proj(x2, w_bf, cos_t, sin_t, *, q_cols, rope_cols, plain_cols):
    n, d = x2.shape
    cols = w_bf.shape[1]
    tm = min(512, n)
    tn = min(1024, q_cols)
    kern = functools.partial(_in_proj_kernel, q_tiles=q_cols // tn, rope_tiles=rope_cols // tn,
                             plain_tiles=plain_cols // tn)
    return pl.pallas_call(
        kern,
        grid=(n // tm, cols // tn),
        in_specs=[
            pl.BlockSpec((tm, d), lambda i, j: (i, 0)),
            pl.BlockSpec((d, tn), lambda i, j: (0, j)),
            pl.BlockSpec((tm, LANES), lambda i, j: (i, 0)),
            pl.BlockSpec((tm, LANES), lambda i, j: (i, 0)),
        ],
        out_specs=pl.BlockSpec((tm, tn), lambda i, j: (i, j)),
        out_shape=jax.ShapeDtypeStruct((n, cols), BF16),
        compiler_params=_params("parallel", "arbitrary"),
        name="in_proj",
    )(x2, w_bf, cos_t, sin_t)


ONES_ROWS = 16


def _attn_kernel(lq_ref, g_ref, q_ref, k_ref, v_ref, o_ref, vt_ref, sa_ref, sb_ref, m_ref, acc_ref,
                 *, lam_init):
    qi = pl.program_id(2)
    t = q_ref.shape[0]
    seq = k_ref.shape[0]

    @pl.when(qi == 0)
    def _():
        for r in range(seq // t):
            rows = slice(r * t, (r + 1) * t)
            vt_ref[0:V_DIM, rows] = v_ref[rows, :].astype(F32).T.astype(BF16)
        vt_ref[V_DIM:V_DIM + ONES_ROWS, :] = jnp.ones((ONES_ROWS, seq), BF16)

    q = q_ref[...]
    m_ref[...] = jnp.full(m_ref.shape, NEG_BIG, F32)
    acc_ref[...] = jnp.zeros(acc_ref.shape, F32)

    def scores(ki, s_ref):
        k = k_ref[pl.ds(pl.multiple_of(ki * t, t), t), :]
        for c in range(2):
            cs = slice(c * QK_DIM, (c + 1) * QK_DIM)
            s_ref[c] = lax.dot_general(k[:, cs], q[:, cs], (((1,), (1,)), ((), ())),
                                       preferred_element_type=F32)

    def softmax_pv(ki, s_ref, masked):
        vt = vt_ref[:, pl.ds(pl.multiple_of(ki * t, t), t)]
        if masked:
            kpos = lax.broadcasted_iota(jnp.int32, (t, t), 0)
            qpos = lax.broadcasted_iota(jnp.int32, (t, t), 1)
            keep = kpos <= qpos
        for c in range(2):
            st = s_ref[c]
            if masked:
                st = jnp.where(keep, st, NEG_BIG)
            m_prev = m_ref[c:c + 1, :]
            m_new = jnp.maximum(m_prev, jnp.max(st, axis=0, keepdims=True))
            alpha = jnp.exp2(m_prev - m_new)
            p = jnp.exp2(st - m_new).astype(BF16)
            acc_ref[c] = alpha * acc_ref[c] + jnp.dot(vt, p, preferred_element_type=F32)
            m_ref[c:c + 1, :] = m_new

    scores(0, sa_ref)

    def pair(i, carry):
        scores(2 * i + 1, sb_ref)
        softmax_pv(2 * i, sa_ref, False)
        scores(2 * i + 2, sa_ref)
        softmax_pv(2 * i + 1, sb_ref, False)
        return carry

    lax.fori_loop(0, qi // 2, pair, 0)

    @pl.when(qi % 2 == 0)
    def _():
        softmax_pv(qi, sa_ref, True)

    @pl.when(qi % 2 == 1)
    def _():
        scores(qi, sb_ref)
        softmax_pv(qi - 1, sa_ref, False)
        softmax_pv(qi, sb_ref, True)

    lq = lq_ref[...]
    lam = (jnp.exp(jnp.sum(lq[0:1] * lq[1:2], axis=1, keepdims=True))
           - jnp.exp(jnp.sum(lq[2:3] * lq[3:4], axis=1, keepdims=True)) + lam_init)
    o1 = acc_ref[0, 0:V_DIM, :] / acc_ref[0, V_DIM:V_DIM + 1, :]
    o2 = acc_ref[1, 0:V_DIM, :] / acc_ref[1, V_DIM:V_DIM + 1, :]
    out_t = o1 - lam * o2
    out_t = out_t * lax.rsqrt(jnp.mean(out_t * out_t, axis=0, keepdims=True) + RMS_EPS)
    out = out_t.T * g_ref[...] * (1.0 - lam_init)
    o_ref[...] = out.astype(o_ref.dtype)


def _diff_attn(proj, lam_qk, subln_g, *, bsz, seq, heads, lam_init):
    n = proj.shape[0]
    t = min(512, seq)
    nq = seq // t
    kern = functools.partial(_attn_kernel, lam_init=lam_init)
    return pl.pallas_call(
        kern,
        grid=(bsz, heads, nq),
        in_specs=[
            pl.BlockSpec((4, QK_DIM), lambda b, h, qi: (0, 0)),
            pl.BlockSpec((1, V_DIM), lambda b, h, qi: (0, 0)),
            pl.BlockSpec((t, HEAD_COLS), lambda b, h, qi: (b * nq + qi, h)),
            pl.BlockSpec((seq, HEAD_COLS), lambda b, h, qi: (b, heads + h)),
            pl.BlockSpec((seq, V_DIM), lambda b, h, qi: (b, 2 * heads + h)),
        ],
        out_specs=pl.BlockSpec((t, V_DIM), lambda b, h, qi: (b * nq + qi, h)),
        out_shape=jax.ShapeDtypeStruct((n, heads * V_DIM), BF16),
        scratch_shapes=[
            pltpu.VMEM((V_DIM + ONES_ROWS, seq), BF16),
            pltpu.VMEM((2, t, t), F32), pltpu.VMEM((2, t, t), F32),
            pltpu.VMEM((2, t), F32), pltpu.VMEM((2, V_DIM + ONES_ROWS, t), F32),
        ],
        compiler_params=_params("parallel", "parallel", "arbitrary"),
        name="diff_attn",
    )(lam_qk, subln_g.reshape(1, V_DIM), proj, proj, proj)


def _cmul(ar, ai, br, bi):
    return ar * br - ai * bi, ar * bi + ai * br


def _ssm_prep_kernel(are_ref, aim_ref, ldt_ref, bre_ref, bim_ref,
                     bbre_ref, bbim_ref, pre_ref, pim_ref, hre_ref, him_ref):
    a_re = are_ref[...]
    a_im = aim_ref[...]
    dt = jnp.exp(ldt_ref[...])
    mag = jnp.exp(dt * a_re)
    ab_re = mag * jnp.cos(dt * a_im)
    ab_im = mag * jnp.sin(dt * a_im)
    den = a_re * a_re + a_im * a_im
    nr, ni = ab_re - 1.0, ab_im
    z_re = (nr * a_re + ni * a_im) / den
    z_im = (ni * a_re - nr * a_im) / den
    b_re = bre_ref[...]
    b_im = bim_ref[...]
    bbre_ref[...] = z_re * b_re - z_im * b_im
    bbim_ref[...] = z_re * b_im + z_im * b_re
    sub = pre_ref.shape[0]
    pre_ref[0:1, :] = ab_re
    pim_ref[0:1, :] = ab_im
    filled = 1
    while filled < sub:
        tr = pre_ref[filled - 1:filled, :]
        ti = pim_ref[filled - 1:filled, :]
        nr2, ni2 = _cmul(pre_ref[0:filled, :], pim_ref[0:filled, :], tr, ti)
        pre_ref[filled:2 * filled, :] = nr2
        pim_ref[filled:2 * filled, :] = ni2
        filled *= 2
    er = pre_ref[sub - 1:sub, :]
    ei = pim_ref[sub - 1:sub, :]
    for r in range(3):
        hre_ref[r:r + 1, :] = er
        him_ref[r:r + 1, :] = ei
        er, ei = _cmul(er, ei, er, ei)


def _ssm_prep(a_re, a_im, log_dt, b_re, b_im, *, sub):
    g, p = a_re.shape
    w = g * p
    row = lambda t: t.reshape(1, w)
    ldt = jnp.broadcast_to(log_dt[:, None], (g, p))
    tb = lambda t: jnp.transpose(t, (2, 0, 1)).reshape(SSM_GROUP, w)
    full = lambda r: pl.BlockSpec((r, w), lambda: (0, 0))
    shapes = [(SSM_GROUP, w), (SSM_GROUP, w), (sub, w), (sub, w), (3, w), (3, w)]
    return pl.pallas_call(
        _ssm_prep_kernel,
        in_specs=[full(1), full(1), full(1), full(SSM_GROUP), full(SSM_GROUP)],
        out_specs=[full(s[0]) for s in shapes],
        out_shape=[jax.ShapeDtypeStruct(s, F32) for s in shapes],
        compiler_params=pltpu.CompilerParams(vmem_limit_bytes=VMEM_LIMIT),
        name="ssm_prep",
    )(row(a_re), row(a_im), row(ldt), tb(b_re), tb(b_im))


def _block_diag_tiles(blocks, per_tile):
    g, r, c = blocks.shape
    t = blocks.reshape(g // per_tile, per_tile, r, c)
    eye = jnp.eye(per_tile, dtype=blocks.dtype)
    out = t[:, :, :, None, :] * eye[None, :, None, :, None]
    return out.reshape(g // per_tile, per_tile * r, per_tile * c)


GROUPS_PER_TILE = LANES // SSM_GROUP
STATES_PER_TILE = GROUPS_PER_TILE * SSM_STATE
SCAN_STRIP = 512


def _ssm_kernel(u_ref, perm_ref, permt_ref, bb_ref, cre_ref, cim_ref, d_ref, wglu_ref,
                are_ref, aim_ref, pre_ref, pim_ref, hre_ref, him_ref,
                o_ref, xr_ref, xi_ref, hr_ref, hi_ref):
    ci = pl.program_id(1)
    chunk = u_ref.shape[0]
    sub = chunk // SUBLANES
    width = xr_ref.shape[1]
    n_tiles = width // STATES_PER_TILE

    @pl.when(ci == 0)
    def _():
        hr_ref[...] = jnp.zeros(hr_ref.shape, F32)
        hi_ref[...] = jnp.zeros(hi_ref.shape, F32)

    u = jnp.dot(perm_ref[...], u_ref[...], preferred_element_type=F32)
    ub = u.astype(BF16)

    for t in range(n_tiles):
        bu = jnp.dot(ub[:, t * LANES:(t + 1) * LANES], bb_ref[t], preferred_element_type=F32)
        sl = slice(t * STATES_PER_TILE, (t + 1) * STATES_PER_TILE)
        xr_ref[:, sl] = bu[:, :STATES_PER_TILE]
        xi_ref[:, sl] = bu[:, STATES_PER_TILE:]

    row8 = lax.broadcasted_iota(jnp.int32, (SUBLANES, SCAN_STRIP), 0)
    for st in range(width // SCAN_STRIP):
        sl = slice(st * SCAN_STRIP, (st + 1) * SCAN_STRIP)
        ar = jnp.broadcast_to(are_ref[:, sl], (SUBLANES, SCAN_STRIP))
        ai = jnp.broadcast_to(aim_ref[:, sl], (SUBLANES, SCAN_STRIP))
        x0r = jnp.where(row8 == 0, jnp.broadcast_to(hr_ref[:, sl], (SUBLANES, SCAN_STRIP)), 0.0)
        x0i = jnp.where(row8 == 0, jnp.broadcast_to(hi_ref[:, sl], (SUBLANES, SCAN_STRIP)), 0.0)

        def local_scan(j, carry):
            xr, xi = carry
            r0 = pl.multiple_of(j * SUBLANES, SUBLANES)
            nr = ar * xr - ai * xi + xr_ref[pl.ds(r0, SUBLANES), sl]
            ni = ar * xi + ai * xr + xi_ref[pl.ds(r0, SUBLANES), sl]
            xr_ref[pl.ds(r0, SUBLANES), sl] = nr
            xi_ref[pl.ds(r0, SUBLANES), sl] = ni
            return nr, ni

        er, ei = lax.fori_loop(0, sub, local_scan, (x0r, x0i), unroll=4)

        for r, d in enumerate((1, 2, 4)):
            pr = jnp.broadcast_to(hre_ref[r:r + 1, sl], (SUBLANES, SCAN_STRIP))
            pi = jnp.broadcast_to(him_ref[r:r + 1, sl], (SUBLANES, SCAN_STRIP))
            sr = jnp.where(row8 >= d, pltpu.roll(er, d, 0), 0.0)
            si = jnp.where(row8 >= d, pltpu.roll(ei, d, 0), 0.0)
            er, ei = er + pr * sr - pi * si, ei + pr * si + pi * sr
        hr_ref[:, sl] = er[SUBLANES - 1:SUBLANES, :]
        hi_ref[:, sl] = ei[SUBLANES - 1:SUBLANES, :]
        cr = jnp.where(row8 >= 1, pltpu.roll(er, 1, 0), 0.0)
        cim = jnp.where(row8 >= 1, pltpu.roll(ei, 1, 0), 0.0)

        def fixup(j, carry):
            r0 = pl.multiple_of(j * SUBLANES, SUBLANES)
            pr = jnp.broadcast_to(pre_ref[pl.ds(j, 1), sl], (SUBLANES, SCAN_STRIP))
            pi = jnp.broadcast_to(pim_ref[pl.ds(j, 1), sl], (SUBLANES, SCAN_STRIP))
            nr = xr_ref[pl.ds(r0, SUBLANES), sl] + pr * cr - pi * cim
            ni = xi_ref[pl.ds(r0, SUBLANES), sl] + pr * cim + pi * cr
            xr_ref[pl.ds(r0, SUBLANES), sl] = nr
            xi_ref[pl.ds(r0, SUBLANES), sl] = ni
            return carry

        lax.fori_loop(0, sub, fixup, 0, unroll=4)

    ys = []
    for t in range(n_tiles):
        sl = slice(t * STATES_PER_TILE, (t + 1) * STATES_PER_TILE)
        ys.append(jnp.dot(xr_ref[:, sl].astype(BF16), cre_ref[t], preferred_element_type=F32)
                  - jnp.dot(xi_ref[:, sl].astype(BF16), cim_ref[t], preferred_element_type=F32))
    y = jnp.concatenate(ys, axis=1) + d_ref[...] * u
    g = jax.nn.gelu(y)
    z = jnp.dot(g.astype(BF16), wglu_ref[...], preferred_element_type=F32)
    out = (g * jax.nn.sigmoid(z)).astype(BF16)
    o_ref[...] = jnp.dot(permt_ref[...], out, preferred_element_type=F32).astype(o_ref.dtype)


def _s5_ssm(proj, u_col_block, prep, c_re, c_im, d_skip, w_glu_bf, *, bsz, seq, chunk):
    bb_re, bb_im, pow_re, pow_im, hs_re, hs_im = prep
    ab_re, ab_im = pow_re[0:1], pow_im[0:1]
    n = proj.shape[0]
    groups = c_re.shape[0]
    ssm_w = groups * SSM_GROUP
    width = groups * SSM_STATE
    n_tiles = ssm_w // LANES
    sub = chunk // SUBLANES
    nc = seq // chunk
    to_blocks = lambda t: jnp.transpose(t.reshape(SSM_GROUP, groups, SSM_STATE), (1, 0, 2))
    bb = jnp.concatenate([_block_diag_tiles(to_blocks(bb_re), GROUPS_PER_TILE),
                          _block_diag_tiles(to_blocks(bb_im), GROUPS_PER_TILE)], axis=2).astype(BF16)
    ct = lambda t: _block_diag_tiles(jnp.transpose(t, (0, 2, 1)), GROUPS_PER_TILE).astype(BF16)
    r = jnp.arange(chunk)
    src = (r % SUBLANES) * sub + r // SUBLANES
    perm = (src[:, None] == jnp.arange(chunk)[None, :]).astype(BF16)
    const2 = lambda shape: pl.BlockSpec(shape, lambda b, c: (0, 0))
    const3 = lambda shape: pl.BlockSpec(shape, lambda b, c: (0, 0, 0))
    return pl.pallas_call(
        _ssm_kernel,
        grid=(bsz, nc),
        in_specs=[
            pl.BlockSpec((chunk, ssm_w), lambda b, c: (b * nc + c, u_col_block)),
            const2((chunk, chunk)), const2((chunk, chunk)),
            const3((n_tiles, LANES, 2 * STATES_PER_TILE)),
            const3((n_tiles, STATES_PER_TILE, LANES)), const3((n_tiles, STATES_PER_TILE, LANES)),
            const2((1, ssm_w)), const2((ssm_w, ssm_w)),
            const2((1, width)), const2((1, width)),
            const2((sub, width)), const2((sub, width)),
            const2((3, width)), const2((3, width)),
        ],
        out_specs=pl.BlockSpec((chunk, ssm_w), lambda b, c: (b * nc + c, 0)),
        out_shape=jax.ShapeDtypeStruct((n, ssm_w), BF16),
        scratch_shapes=[
            pltpu.VMEM((chunk, width), F32), pltpu.VMEM((chunk, width), F32),
            pltpu.VMEM((1, width), F32), pltpu.VMEM((1, width), F32),
        ],
        compiler_params=_params("parallel", "arbitrary"),
        name="s5_ssm",
    )(proj, perm, perm.T, bb, ct(c_re), ct(c_im), d_skip.reshape(1, ssm_w), w_glu_bf,
      ab_re, ab_im, pow_re, pow_im, hs_re, hs_im)


def _layer_norm(y, g, b):
    mu = jnp.mean(y, axis=1, keepdims=True)
    yc = y - mu
    var = jnp.mean(yc * yc, axis=1, keepdims=True)
    return yc * lax.rsqrt(var + LN_EPS) * g + b


def _mix_kernel(x_ref, attn_ref, ssm_ref, ga_ref, gs_ref, wa_ref, ws_ref, wo_ref, g_ref, b_ref,
                o_ref, *, alpha):
    a = jnp.dot(attn_ref[...], wa_ref[...], preferred_element_type=F32)
    s = jnp.dot(ssm_ref[...], ws_ref[...], preferred_element_type=F32)
    merged = ga_ref[...].astype(F32) * a + gs_ref[...].astype(F32) * s
    mix = jnp.dot(merged.astype(BF16), wo_ref[...], preferred_element_type=F32)
    o_ref[...] = _layer_norm(alpha * x_ref[...] + mix, g_ref[...], b_ref[...])


def _mix_out(x2, attn, ssm, proj, ga_block, wa, ws, wo, ln_g, ln_b, *, alpha):
    n, d = x2.shape
    aw = attn.shape[1]
    sw = ssm.shape[1]
    tm = min(256, n)
    row = lambda w: pl.BlockSpec((tm, w), lambda i: (i, 0))
    const = lambda shape: pl.BlockSpec(shape, lambda i: (0, 0), pipeline_mode=pl.Buffered(1))
    return pl.pallas_call(
        functools.partial(_mix_kernel, alpha=alpha),
        grid=(n // tm,),
        in_specs=[
            row(d), row(aw), row(sw),
            pl.BlockSpec((tm, d), lambda i: (i, ga_block)),
            pl.BlockSpec((tm, d), lambda i: (i, ga_block + 1)),
            const((aw, d)), const((sw, d)), const((d, d)), const((1, d)), const((1, d)),
        ],
        out_specs=row(d),
        out_shape=jax.ShapeDtypeStruct((n, d), F32),
        compiler_params=_params("parallel"),
        name="mix_out",
    )(x2, attn, ssm, proj, proj, wa, ws, wo, ln_g.reshape(1, d), ln_b.reshape(1, d))


def _mlp_kernel(x_ref, w1_ref, w2_ref, g_ref, b_ref, o_ref, xb_ref, acc_ref, *, alpha):
    f = pl.program_id(1)

    @pl.when(f == 0)
    def _():
        xb_ref[...] = x_ref[...].astype(BF16)
        acc_ref[...] = jnp.zeros(acc_ref.shape, F32)

    h = jnp.dot(xb_ref[...], w1_ref[...], preferred_element_type=F32)
    h = jnp.square(jnp.maximum(h, 0.0))
    acc_ref[...] += jnp.dot(h.astype(BF16), w2_ref[...], preferred_element_type=F32)

    @pl.when(f == pl.num_programs(1) - 1)
    def _():
        o_ref[...] = _layer_norm(alpha * x_ref[...] + acc_ref[...], g_ref[...], b_ref[...])


def _mlp(x2, w1, w2, ln_g, ln_b, *, alpha):
    n, d = x2.shape
    dff = w1.shape[1]
    tm = min(512, n)
    tf = min(512, dff)
    return pl.pallas_call(
        functools.partial(_mlp_kernel, alpha=alpha),
        grid=(n // tm, dff // tf),
        in_specs=[
            pl.BlockSpec((tm, d), lambda i, f: (i, 0)),
            pl.BlockSpec((d, tf), lambda i, f: (0, f)),
            pl.BlockSpec((tf, d), lambda i, f: (f, 0)),
            pl.BlockSpec((1, d), lambda i, f: (0, 0)),
            pl.BlockSpec((1, d), lambda i, f: (0, 0)),
        ],
        out_specs=pl.BlockSpec((tm, d), lambda i, f: (i, 0)),
        out_shape=jax.ShapeDtypeStruct((n, d), F32),
        scratch_shapes=[pltpu.VMEM((tm, d), BF16), pltpu.VMEM((tm, d), F32)],
        compiler_params=_params("parallel", "arbitrary"),
        name="mlp",
    )(x2, w1, w2, ln_g.reshape(1, d), ln_b.reshape(1, d))


def _rope_tables(positions):
    inv = ROPE_THETA ** (-jnp.arange(0, QK_DIM, 2, dtype=F32) / QK_DIM)
    ang = positions.astype(F32).reshape(-1, 1) * inv
    cos = jnp.cos(ang)
    sin = jnp.sin(ang)
    reps = LANES // QK_DIM
    return (jnp.tile(jnp.concatenate([cos, cos], axis=1), (1, reps)),
            jnp.tile(jnp.concatenate([-sin, sin], axis=1), (1, reps)))


def kernel(x, positions, w_in, lambda_qk, subln_g, ssm_a_re, ssm_a_im, ssm_log_dt, ssm_b_re,
           ssm_b_im, ssm_c_re, ssm_c_im, ssm_d, w_glu, w_attn_up, w_ssm_up, w_out, ln1_g, ln1_b,
           ln2_g, ln2_b, w_mlp_up, w_mlp_down):
    bsz, seq, d = x.shape
    depth = w_in.shape[0]
    attn_w = w_attn_up.shape[1]
    ssm_w = w_ssm_up.shape[1]
    heads = attn_w // V_DIM
    q_cols = heads * HEAD_COLS
    assert w_in.shape[2] == 2 * q_cols + attn_w + ssm_w + 2 * d
    assert q_cols == attn_w == ssm_w and d % ssm_w == 0
    alpha = (2.0 * depth) ** 0.25
    chunk = min(512, seq)

    cos_t, sin_t = _rope_tables(positions)
    x2 = x.reshape(bsz * seq, d)
    for l in range(depth):
        lam_init = _lambda_init(l)
        proj = _in_proj(x2, w_in[l].astype(BF16), cos_t, sin_t, q_cols=q_cols,
                        rope_cols=2 * q_cols, plain_cols=attn_w + ssm_w)
        attn = _diff_attn(proj, lambda_qk[l], subln_g[l], bsz=bsz, seq=seq, heads=heads,
                          lam_init=lam_init)
        prep = _ssm_prep(ssm_a_re[l], ssm_a_im[l], ssm_log_dt[l], ssm_b_re[l], ssm_b_im[l],
                         sub=chunk // SUBLANES)
        ssm = _s5_ssm(proj, (2 * q_cols + attn_w) // ssm_w, prep, ssm_c_re[l], ssm_c_im[l],
                      ssm_d[l], w_glu[l].astype(BF16), bsz=bsz, seq=seq, chunk=chunk)
        x2 = _mix_out(x2, attn, ssm, proj, (2 * q_cols + attn_w + ssm_w) // d,
                      w_attn_up[l].astype(BF16), w_ssm_up[l].astype(BF16), w_out[l].astype(BF16),
                      ln1_g[l], ln1_b[l], alpha=alpha)
        x2 = _mlp(x2, w_mlp_up[l].astype(BF16), w_mlp_down[l].astype(BF16), ln2_g[l], ln2_b[l],
                  alpha=alpha)
    return x2.reshape(bsz, seq, d)
```

```python
import functools
import math

import jax
import jax.numpy as jnp
from jax import lax
from jax.experimental import pallas as pl
from jax.experimental.pallas import tpu as pltpu

F32 = jnp.float32
BF16 = jnp.bfloat16

QK_DIM = 64
V_DIM = 2 * QK_DIM
HEAD_COLS = 2 * QK_DIM
SSM_GROUP = 16
SSM_STATE = 64
ROPE_THETA = 10000.0
LN_EPS = 1e-5
RMS_EPS = 1e-5
LANES = 128
SUBLANES = 8
VMEM_LIMIT = 56 * 1024 * 1024
NEG_BIG = -1e30


def _lambda_init(layer):
    return 0.8 - 0.6 * math.exp(-0.3 * layer)


def _params(*sem):
    return pltpu.CompilerParams(dimension_semantics=sem, vmem_limit_bytes=VMEM_LIMIT)


EPILOGUE_COLS = 256


def _in_proj_kernel(x_ref, w_ref, cos_ref, sin_ref, o_ref, xb_ref, *, q_tiles, rope_tiles,
                    plain_tiles):
    j = pl.program_id(1)
    tn = o_ref.shape[1]
    chunks = [slice(c0, c0 + EPILOGUE_COLS) for c0 in range(0, tn, EPILOGUE_COLS)]

    @pl.when(j == 0)
    def _():
        xb_ref[...] = x_ref[...].astype(BF16)

    def matmul(cs):
        return jnp.dot(xb_ref[...], w_ref[:, cs], preferred_element_type=F32)

    @pl.when(j < rope_tiles)
    def _():
        scale = jnp.where(j < q_tiles, QK_DIM ** -0.5 * math.log2(math.e), 1.0).astype(F32)
        reps = EPILOGUE_COLS // LANES
        c = jnp.tile(cos_ref[...] * scale, (1, reps))
        s = jnp.tile(sin_ref[...] * scale, (1, reps))
        lane = lax.broadcasted_iota(jnp.int32, c.shape, 1)
        first_half = (lane % QK_DIM) < (QK_DIM // 2)
        for cs in chunks:
            acc = matmul(cs)
            rot = jnp.where(first_half, pltpu.roll(acc, EPILOGUE_COLS - QK_DIM // 2, 1),
                            pltpu.roll(acc, QK_DIM // 2, 1))
            o_ref[:, cs] = (acc * c + rot * s).astype(o_ref.dtype)

    @pl.when((j >= rope_tiles) & (j < rope_tiles + plain_tiles))
    def _():
        for cs in chunks:
            o_ref[:, cs] = matmul(cs).astype(o_ref.dtype)

    @pl.when(j >= rope_tiles + plain_tiles)
    def _():
        for cs in chunks:
            o_ref[:, cs] = jax.nn.sigmoid(matmul(cs)).astype(o_ref.dtype)


def _in_proj(x2, w_bf, cos_t, sin_t, *, q_cols, rope_cols, plain_cols):
    n, d = x2.shape
    cols = w_bf.shape[1]
    tm = min(1024, n)
    tn = min(1024, q_cols)
    kern = functools.partial(_in_proj_kernel, q_tiles=q_cols // tn, rope_tiles=rope_cols // tn,
                             plain_tiles=plain_cols // tn)
    return pl.pallas_call(
        kern,
        grid=(n // tm, cols // tn),
        in_specs=[
            pl.BlockSpec((tm, d), lambda i, j: (i, 0)),
            pl.BlockSpec((d, tn), lambda i, j: (0, j)),
            pl.BlockSpec((tm, LANES), lambda i, j: (i, 0)),
            pl.BlockSpec((tm, LANES), lambda i, j: (i, 0)),
        ],
        out_specs=pl.BlockSpec((tm, tn), lambda i, j: (i, j)),
        out_shape=jax.ShapeDtypeStruct((n, cols), BF16),
        scratch_shapes=[pltpu.VMEM((tm, d), BF16)],
        compiler_params=_params("parallel", "arbitrary"),
        name="in_proj",
    )(x2, w_bf, cos_t, sin_t)


ONES_ROWS = 16


def _attn_kernel(lq_ref, g_ref, q_ref, k_ref, v_ref, o_ref, vt_ref, sa_ref, sb_ref, bma_ref, bmb_ref,
                 m_ref, acc_ref, *, lam_init):
    qi = pl.program_id(2)
    t = q_ref.shape[0]
    seq = k_ref.shape[0]

    @pl.when(qi == 0)
    def _():
        for r in range(seq // t):
            rows = slice(r * t, (r + 1) * t)
            vt_ref[0:V_DIM, rows] = v_ref[rows, :].astype(F32).T.astype(BF16)
        vt_ref[V_DIM:V_DIM + ONES_ROWS, :] = jnp.ones((ONES_ROWS, seq), BF16)

    q = q_ref[...]
    m_ref[...] = jnp.full(m_ref.shape, NEG_BIG, F32)
    acc_ref[...] = jnp.zeros(acc_ref.shape, F32)

    def scores(ki, buf, masked):
        s_ref, bm_ref = buf
        k = k_ref[pl.ds(pl.multiple_of(ki * t, t), t), :]
        if masked:
            kpos = lax.broadcasted_iota(jnp.int32, (t, t), 0)
            qpos = lax.broadcasted_iota(jnp.int32, (t, t), 1)
            keep = kpos <= qpos
        for c in range(2):
            cs = slice(c * QK_DIM, (c + 1) * QK_DIM)
            st = lax.dot_general(k[:, cs], q[:, cs], (((1,), (1,)), ((), ())),
                                 preferred_element_type=F32)
            if masked:
                st = jnp.where(keep, st, NEG_BIG)
            s_ref[c] = st
            bm_ref[c:c + 1, :] = jnp.max(st, axis=0, keepdims=True)

    def softmax_pv(ki, buf):
        s_ref, bm_ref = buf
        vt = vt_ref[:, pl.ds(pl.multiple_of(ki * t, t), t)]
        for c in range(2):
            m_prev = m_ref[c:c + 1, :]
            m_new = jnp.maximum(m_prev, bm_ref[c:c + 1, :])
            alpha = jnp.exp2(m_prev - m_new)
            p = jnp.exp2(s_ref[c] - m_new).astype(BF16)
            acc_ref[c] = alpha * acc_ref[c] + jnp.dot(vt, p, preferred_element_type=F32)
            m_ref[c:c + 1, :] = m_new

    buf_a = (sa_ref, bma_ref)
    buf_b = (sb_ref, bmb_ref)

    @pl.when(qi == 0)
    def _():
        scores(0, buf_a, True)

    @pl.when(qi > 0)
    def _():
        scores(0, buf_a, False)

    def pair(i, carry):
        scores(2 * i + 1, buf_b, False)
        softmax_pv(2 * i, buf_a)
        scores(2 * i + 2, buf_a, False)
        softmax_pv(2 * i + 1, buf_b)
        return carry

    n_pairs = jnp.maximum(qi - 1, 0) // 2
    lax.fori_loop(0, n_pairs, pair, 0)
    done = 2 * n_pairs

    @pl.when(qi - done == 0)
    def _():
        softmax_pv(qi, buf_a)

    @pl.when(qi - done == 1)
    def _():
        scores(qi, buf_b, True)
        softmax_pv(done, buf_a)
        softmax_pv(qi, buf_b)

    @pl.when(qi - done == 2)
    def _():
        scores(done + 1, buf_b, False)
        softmax_pv(done, buf_a)
        scores(qi, buf_a, True)
        softmax_pv(done + 1, buf_b)
        softmax_pv(qi, buf_a)

    lq = lq_ref[...]
    lam = (jnp.exp(jnp.sum(lq[0:1] * lq[1:2], axis=1, keepdims=True))
           - jnp.exp(jnp.sum(lq[2:3] * lq[3:4], axis=1, keepdims=True)) + lam_init)
    o1 = acc_ref[0, 0:V_DIM, :] / acc_ref[0, V_DIM:V_DIM + 1, :]
    o2 = acc_ref[1, 0:V_DIM, :] / acc_ref[1, V_DIM:V_DIM + 1, :]
    out_t = o1 - lam * o2
    out_t = out_t * lax.rsqrt(jnp.mean(out_t * out_t, axis=0, keepdims=True) + RMS_EPS)
    out = out_t.T * g_ref[...] * (1.0 - lam_init)
    o_ref[...] = out.astype(o_ref.dtype)


def _diff_attn(proj, lam_qk, subln_g, *, bsz, seq, heads, lam_init):
    n = proj.shape[0]
    t = min(512, seq)
    nq = seq // t
    kern = functools.partial(_attn_kernel, lam_init=lam_init)
    return pl.pallas_call(
        kern,
        grid=(bsz, heads, nq),
        in_specs=[
            pl.BlockSpec((4, QK_DIM), lambda b, h, qi: (0, 0)),
            pl.BlockSpec((1, V_DIM), lambda b, h, qi: (0, 0)),
            pl.BlockSpec((t, HEAD_COLS), lambda b, h, qi: (b * nq + qi, h)),
            pl.BlockSpec((seq, HEAD_COLS), lambda b, h, qi: (b, heads + h)),
            pl.BlockSpec((seq, V_DIM), lambda b, h, qi: (b, 2 * heads + h)),
        ],
        out_specs=pl.BlockSpec((t, V_DIM), lambda b, h, qi: (b * nq + qi, h)),
        out_shape=jax.ShapeDtypeStruct((n, heads * V_DIM), BF16),
        scratch_shapes=[
            pltpu.VMEM((V_DIM + ONES_ROWS, seq), BF16),
            pltpu.VMEM((2, t, t), F32), pltpu.VMEM((2, t, t), F32),
            pltpu.VMEM((2, t), F32), pltpu.VMEM((2, t), F32),
            pltpu.VMEM((2, t), F32), pltpu.VMEM((2, V_DIM + ONES_ROWS, t), F32),
        ],
        compiler_params=_params("parallel", "parallel", "arbitrary"),
        name="diff_attn",
    )(lam_qk, subln_g.reshape(1, V_DIM), proj, proj, proj)


def _cmul(ar, ai, br, bi):
    return ar * br - ai * bi, ar * bi + ai * br


def _ssm_prep_kernel(are_ref, aim_ref, ldt_ref, bre_ref, bim_ref,
                     bbre_ref, bbim_ref, pre_ref, pim_ref, hre_ref, him_ref):
    a_re = are_ref[...]
    a_im = aim_ref[...]
    dt = jnp.exp(ldt_ref[...])
    mag = jnp.exp(dt * a_re)
    ab_re = mag * jnp.cos(dt * a_im)
    ab_im = mag * jnp.sin(dt * a_im)
    den = a_re * a_re + a_im * a_im
    nr, ni = ab_re - 1.0, ab_im
    z_re = (nr * a_re + ni * a_im) / den
    z_im = (ni * a_re - nr * a_im) / den
    b_re = bre_ref[...]
    b_im = bim_ref[...]
    bbre_ref[...] = z_re * b_re - z_im * b_im
    bbim_ref[...] = z_re * b_im + z_im * b_re
    sub = pre_ref.shape[0]
    pre_ref[0:1, :] = ab_re
    pim_ref[0:1, :] = ab_im
    filled = 1
    while filled < sub:
        tr = pre_ref[filled - 1:filled, :]
        ti = pim_ref[filled - 1:filled, :]
        nr2, ni2 = _cmul(pre_ref[0:filled, :], pim_ref[0:filled, :], tr, ti)
        pre_ref[filled:2 * filled, :] = nr2
        pim_ref[filled:2 * filled, :] = ni2
        filled *= 2
    er = pre_ref[sub - 1:sub, :]
    ei = pim_ref[sub - 1:sub, :]
    for r in range(3):
        hre_ref[r:r + 1, :] = er
        him_ref[r:r + 1, :] = ei
        er, ei = _cmul(er, ei, er, ei)


def _ssm_prep(a_re, a_im, log_dt, b_re, b_im, *, sub):
    g, p = a_re.shape
    w = g * p
    row = lambda t: t.reshape(1, w)
    ldt = jnp.broadcast_to(log_dt[:, None], (g, p))
    tb = lambda t: jnp.transpose(t, (2, 0, 1)).reshape(SSM_GROUP, w)
    full = lambda r: pl.BlockSpec((r, w), lambda: (0, 0))
    shapes = [(SSM_GROUP, w), (SSM_GROUP, w), (sub, w), (sub, w), (3, w), (3, w)]
    return pl.pallas_call(
        _ssm_prep_kernel,
        in_specs=[full(1), full(1), full(1), full(SSM_GROUP), full(SSM_GROUP)],
        out_specs=[full(s[0]) for s in shapes],
        out_shape=[jax.ShapeDtypeStruct(s, F32) for s in shapes],
        compiler_params=pltpu.CompilerParams(vmem_limit_bytes=VMEM_LIMIT),
        name="ssm_prep",
    )(row(a_re), row(a_im), row(ldt), tb(b_re), tb(b_im))


def _block_diag_tiles(blocks, per_tile):
    g, r, c = blocks.shape
    t = blocks.reshape(g // per_tile, per_tile, r, c)
    eye = jnp.eye(per_tile, dtype=blocks.dtype)
    out = t[:, :, :, None, :] * eye[None, :, None, :, None]
    return out.reshape(g // per_tile, per_tile * r, per_tile * c)


GROUPS_PER_TILE = LANES // SSM_GROUP
STATES_PER_TILE = GROUPS_PER_TILE * SSM_STATE
SCAN_STRIP = 512


def _ssm_kernel(u_ref, perm_ref, permt_ref, bb_ref, cre_ref, cim_ref, d_ref, wglu_ref,
                are_ref, aim_ref, pre_ref, pim_ref, hre_ref, him_ref,
                o_ref, xr_ref, xi_ref, hr_ref, hi_ref):
    ci = pl.program_id(1)
    chunk = u_ref.shape[0]
    sub = chunk // SUBLANES
    width = xr_ref.shape[1]
    n_tiles = width // STATES_PER_TILE

    @pl.when(ci == 0)
    def _():
        hr_ref[...] = jnp.zeros(hr_ref.shape, F32)
        hi_ref[...] = jnp.zeros(hi_ref.shape, F32)

    u = jnp.dot(perm_ref[...], u_ref[...], preferred_element_type=F32)
    ub = u.astype(BF16)

    for t in range(n_tiles):
        bu = jnp.dot(ub[:, t * LANES:(t + 1) * LANES], bb_ref[t], preferred_element_type=F32)
        sl = slice(t * STATES_PER_TILE, (t + 1) * STATES_PER_TILE)
        xr_ref[:, sl] = bu[:, :STATES_PER_TILE]
        xi_ref[:, sl] = bu[:, STATES_PER_TILE:]

    row8 = lax.broadcasted_iota(jnp.int32, (SUBLANES, SCAN_STRIP), 0)
    for st in range(width // SCAN_STRIP):
        sl = slice(st * SCAN_STRIP, (st + 1) * SCAN_STRIP)
        ar = jnp.broadcast_to(are_ref[:, sl], (SUBLANES, SCAN_STRIP))
        ai = jnp.broadcast_to(aim_ref[:, sl], (SUBLANES, SCAN_STRIP))
        x0r = jnp.where(row8 == 0, jnp.broadcast_to(hr_ref[:, sl], (SUBLANES, SCAN_STRIP)), 0.0)
        x0i = jnp.where(row8 == 0, jnp.broadcast_to(hi_ref[:, sl], (SUBLANES, SCAN_STRIP)), 0.0)

        def local_scan(j, carry):
            xr, xi = carry
            r0 = pl.multiple_of(j * SUBLANES, SUBLANES)
            nr = ar * xr - ai * xi + xr_ref[pl.ds(r0, SUBLANES), sl]
            ni = ar * xi + ai * xr + xi_ref[pl.ds(r0, SUBLANES), sl]
            xr_ref[pl.ds(r0, SUBLANES), sl] = nr
            xi_ref[pl.ds(r0, SUBLANES), sl] = ni
            return nr, ni

        er, ei = lax.fori_loop(0, sub, local_scan, (x0r, x0i), unroll=4)

        for r, d in enumerate((1, 2, 4)):
            pr = jnp.broadcast_to(hre_ref[r:r + 1, sl], (SUBLANES, SCAN_STRIP))
            pi = jnp.broadcast_to(him_ref[r:r + 1, sl], (SUBLANES, SCAN_STRIP))
            sr = jnp.where(row8 >= d, pltpu.roll(er, d, 0), 0.0)
            si = jnp.where(row8 >= d, pltpu.roll(ei, d, 0), 0.0)
            er, ei = er + pr * sr - pi * si, ei + pr * si + pi * sr
        hr_ref[:, sl] = er[SUBLANES - 1:SUBLANES, :]
        hi_ref[:, sl] = ei[SUBLANES - 1:SUBLANES, :]
        cr = jnp.where(row8 >= 1, pltpu.roll(er, 1, 0), 0.0)
        cim = jnp.where(row8 >= 1, pltpu.roll(ei, 1, 0), 0.0)

        def fixup(j, carry):
            r0 = pl.multiple_of(j * SUBLANES, SUBLANES)
            pr = jnp.broadcast_to(pre_ref[pl.ds(j, 1), sl], (SUBLANES, SCAN_STRIP))
            pi = jnp.broadcast_to(pim_ref[pl.ds(j, 1), sl], (SUBLANES, SCAN_STRIP))
            nr = xr_ref[pl.ds(r0, SUBLANES), sl] + pr * cr - pi * cim
            ni = xi_ref[pl.ds(r0, SUBLANES), sl] + pr * cim + pi * cr
            xr_ref[pl.ds(r0, SUBLANES), sl] = nr
            xi_ref[pl.ds(r0, SUBLANES), sl] = ni
            return carry

        lax.fori_loop(0, sub, fixup, 0, unroll=4)

    ys = []
    for t in range(n_tiles):
        sl = slice(t * STATES_PER_TILE, (t + 1) * STATES_PER_TILE)
        ys.append(jnp.dot(xr_ref[:, sl].astype(BF16), cre_ref[t], preferred_element_type=F32)
                  - jnp.dot(xi_ref[:, sl].astype(BF16), cim_ref[t], preferred_element_type=F32))
    y = jnp.concatenate(ys, axis=1) + d_ref[...] * u
    g = jax.nn.gelu(y)
    z = jnp.dot(g.astype(BF16), wglu_ref[...], preferred_element_type=F32)
    out = (g * jax.nn.sigmoid(z)).astype(BF16)
    o_ref[...] = jnp.dot(permt_ref[...], out, preferred_element_type=F32).astype(o_ref.dtype)


def _s5_ssm(proj, u_col_block, prep, c_re, c_im, d_skip, w_glu_bf, *, bsz, seq, chunk):
    bb_re, bb_im, pow_re, pow_im, hs_re, hs_im = prep
    ab_re, ab_im = pow_re[0:1], pow_im[0:1]
    n = proj.shape[0]
    groups = c_re.shape[0]
    ssm_w = groups * SSM_GROUP
    width = groups * SSM_STATE
    n_tiles = ssm_w // LANES
    sub = chunk // SUBLANES
    nc = seq // chunk
    to_blocks = lambda t: jnp.transpose(t.reshape(SSM_GROUP, groups, SSM_STATE), (1, 0, 2))
    bb = jnp.concatenate([_block_diag_tiles(to_blocks(bb_re), GROUPS_PER_TILE),
                          _block_diag_tiles(to_blocks(bb_im), GROUPS_PER_TILE)], axis=2).astype(BF16)
    ct = lambda t: _block_diag_tiles(jnp.transpose(t, (0, 2, 1)), GROUPS_PER_TILE).astype(BF16)
    r = jnp.arange(chunk)
    src = (r % SUBLANES) * sub + r // SUBLANES
    perm = (src[:, None] == jnp.arange(chunk)[None, :]).astype(BF16)
    const2 = lambda shape: pl.BlockSpec(shape, lambda b, c: (0, 0))
    const3 = lambda shape: pl.BlockSpec(shape, lambda b, c: (0, 0, 0))
    return pl.pallas_call(
        _ssm_kernel,
        grid=(bsz, nc),
        in_specs=[
            pl.BlockSpec((chunk, ssm_w), lambda b, c: (b * nc + c, u_col_block)),
            const2((chunk, chunk)), const2((chunk, chunk)),
            const3((n_tiles, LANES, 2 * STATES_PER_TILE)),
            const3((n_tiles, STATES_PER_TILE, LANES)), const3((n_tiles, STATES_PER_TILE, LANES)),
            const2((1, ssm_w)), const2((ssm_w, ssm_w)),
            const2((1, width)), const2((1, width)),
            const2((sub, width)), const2((sub, width)),
            const2((3, width)), const2((3, width)),
        ],
        out_specs=pl.BlockSpec((chunk, ssm_w), lambda b, c: (b * nc + c, 0)),
        out_shape=jax.ShapeDtypeStruct((n, ssm_w), BF16),
        scratch_shapes=[
            pltpu.VMEM((chunk, width), F32), pltpu.VMEM((chunk, width), F32),
            pltpu.VMEM((1, width), F32), pltpu.VMEM((1, width), F32),
        ],
        compiler_params=_params("parallel", "arbitrary"),
        name="s5_ssm",
    )(proj, perm, perm.T, bb, ct(c_re), ct(c_im), d_skip.reshape(1, ssm_w), w_glu_bf,
      ab_re, ab_im, pow_re, pow_im, hs_re, hs_im)


def _layer_norm(y, g, b):
    mu = jnp.mean(y, axis=1, keepdims=True)
    yc = y - mu
    var = jnp.mean(yc * yc, axis=1, keepdims=True)
    return yc * lax.rsqrt(var + LN_EPS) * g + b


def _mix_kernel(x_ref, attn_ref, ssm_ref, ga_ref, gs_ref, wa_ref, ws_ref, wo_ref, g_ref, b_ref,
                o_ref, *, alpha):
    a = jnp.dot(attn_ref[...], wa_ref[...], preferred_element_type=F32)
    s = jnp.dot(ssm_ref[...], ws_ref[...], preferred_element_type=F32)
    merged = ga_ref[...].astype(F32) * a + gs_ref[...].astype(F32) * s
    mix = jnp.dot(merged.astype(BF16), wo_ref[...], preferred_element_type=F32)
    o_ref[...] = _layer_norm(alpha * x_ref[...] + mix, g_ref[...], b_ref[...])


def _mix_out(x2, attn, ssm, proj, ga_block, wa, ws, wo, ln_g, ln_b, *, alpha):
    n, d = x2.shape
    aw = attn.shape[1]
    sw = ssm.shape[1]
    tm = min(256, n)
    row = lambda w: pl.BlockSpec((tm, w), lambda i: (i, 0))
    const = lambda shape: pl.BlockSpec(shape, lambda i: (0, 0), pipeline_mode=pl.Buffered(1))
    return pl.pallas_call(
        functools.partial(_mix_kernel, alpha=alpha),
        grid=(n // tm,),
        in_specs=[
            row(d), row(aw), row(sw),
            pl.BlockSpec((tm, d), lambda i: (i, ga_block)),
            pl.BlockSpec((tm, d), lambda i: (i, ga_block + 1)),
            const((aw, d)), const((sw, d)), const((d, d)), const((1, d)), const((1, d)),
        ],
        out_specs=row(d),
        out_shape=jax.ShapeDtypeStruct((n, d), F32),
        compiler_params=_params("parallel"),
        name="mix_out",
    )(x2, attn, ssm, proj, proj, wa, ws, wo, ln_g.reshape(1, d), ln_b.reshape(1, d))


def _mlp_kernel(x_ref, w1_ref, w2_ref, g_ref, b_ref, o_ref, xb_ref, acc_ref, *, alpha):
    f = pl.program_id(1)

    @pl.when(f == 0)
    def _():
        xb_ref[...] = x_ref[...].astype(BF16)
        acc_ref[...] = jnp.zeros(acc_ref.shape, F32)

    h = jnp.dot(xb_ref[...], w1_ref[...], preferred_element_type=F32)
    h = jnp.square(jnp.maximum(h, 0.0))
    acc_ref[...] += jnp.dot(h.astype(BF16), w2_ref[...], preferred_element_type=F32)

    @pl.when(f == pl.num_programs(1) - 1)
    def _():
        o_ref[...] = _layer_norm(alpha * x_ref[...] + acc_ref[...], g_ref[...], b_ref[...])


def _mlp(x2, w1, w2, ln_g, ln_b, *, alpha):
    n, d = x2.shape
    dff = w1.shape[1]
    tm = min(512, n)
    tf = min(1024, dff)
    return pl.pallas_call(
        functools.partial(_mlp_kernel, alpha=alpha),
        grid=(n // tm, dff // tf),
        in_specs=[
            pl.BlockSpec((tm, d), lambda i, f: (i, 0)),
            pl.BlockSpec((d, tf), lambda i, f: (0, f)),
            pl.BlockSpec((tf, d), lambda i, f: (f, 0)),
            pl.BlockSpec((1, d), lambda i, f: (0, 0)),
            pl.BlockSpec((1, d), lambda i, f: (0, 0)),
        ],
        out_specs=pl.BlockSpec((tm, d), lambda i, f: (i, 0)),
        out_shape=jax.ShapeDtypeStruct((n, d), F32),
        scratch_shapes=[pltpu.VMEM((tm, d), BF16), pltpu.VMEM((tm, d), F32)],
        compiler_params=_params("parallel", "arbitrary"),
        name="mlp",
    )(x2, w1, w2, ln_g.reshape(1, d), ln_b.reshape(1, d))


def _rope_tables(positions):
    inv = ROPE_THETA ** (-jnp.arange(0, QK_DIM, 2, dtype=F32) / QK_DIM)
    ang = positions.astype(F32).reshape(-1, 1) * inv
    cos = jnp.cos(ang)
    sin = jnp.sin(ang)
    reps = LANES // QK_DIM
    return (jnp.tile(jnp.concatenate([cos, cos], axis=1), (1, reps)),
            jnp.tile(jnp.concatenate([-sin, sin], axis=1), (1, reps)))


def kernel(x, positions, w_in, lambda_qk, subln_g, ssm_a_re, ssm_a_im, ssm_log_dt, ssm_b_re,
           ssm_b_im, ssm_c_re, ssm_c_im, ssm_d, w_glu, w_attn_up, w_ssm_up, w_out, ln1_g, ln1_b,
           ln2_g, ln2_b, w_mlp_up, w_mlp_down):
    bsz, seq, d = x.shape
    depth = w_in.shape[0]
    attn_w = w_attn_up.shape[1]
    ssm_w = w_ssm_up.shape[1]
    heads = attn_w // V_DIM
    q_cols = heads * HEAD_COLS
    assert w_in.shape[2] == 2 * q_cols + attn_w + ssm_w + 2 * d
    assert q_cols == attn_w == ssm_w and d % ssm_w == 0
    alpha = (2.0 * depth) ** 0.25
    chunk = min(512, seq)

    cos_t, sin_t = _rope_tables(positions)
    x2 = x.reshape(bsz * seq, d)
    for l in range(depth):
        lam_init = _lambda_init(l)
        proj = _in_proj(x2, w_in[l].astype(BF16), cos_t, sin_t, q_cols=q_cols,
                        rope_cols=2 * q_cols, plain_cols=attn_w + ssm_w)
        attn = _diff_attn(proj, lambda_qk[l], subln_g[l], bsz=bsz, seq=seq, heads=heads,
                          lam_init=lam_init)
        prep = _ssm_prep(ssm_a_re[l], ssm_a_im[l], ssm_log_dt[l], ssm_b_re[l], ssm_b_im[l],
                         sub=chunk // SUBLANES)
        ssm = _s5_ssm(proj, (2 * q_cols + attn_w) // ssm_w, prep, ssm_c_re[l], ssm_c_im[l],
                      ssm_d[l], w_glu[l].astype(BF16), bsz=bsz, seq=seq, chunk=chunk)
        x2 = _mix_out(x2, attn, ssm, proj, (2 * q_cols + attn_w + ssm_w) // d,
                      w_attn_up[l].astype(BF16), w_ssm_up[l].astype(BF16), w_out[l].astype(BF16),
                      ln1_g[l], ln1_b[l], alpha=alpha)
        x2 = _mlp(x2, w_mlp_up[l].astype(BF16), w_mlp_down[l].astype(BF16), ln2_g[l], ln2_b[l],
                  alpha=alpha)
    return x2.reshape(bsz, seq, d)
```

```python
import functools
import math

import jax
import jax.numpy as jnp
from jax import lax
from jax.experimental import pallas as pl
from jax.experimental.pallas import tpu as pltpu

F32 = jnp.float32
BF16 = jnp.bfloat16

QK_DIM = 64
V_DIM = 2 * QK_DIM
HEAD_COLS = 2 * QK_DIM
SSM_GROUP = 16
SSM_STATE = 64
ROPE_THETA = 10000.0
LN_EPS = 1e-5
RMS_EPS = 1e-5
LANES = 128
SUBLANES = 8
VMEM_LIMIT = 56 * 1024 * 1024
NEG_BIG = -1e30


def _lambda_init(layer):
    return 0.8 - 0.6 * math.exp(-0.3 * layer)


def _params(*sem):
    return pltpu.CompilerParams(dimension_semantics=sem, vmem_limit_bytes=VMEM_LIMIT)


EPILOGUE_COLS = 256


def _in_proj_kernel(x_ref, w_ref, cos_ref, sin_ref, o_ref, xb_ref, *, q_tiles, rope_tiles,
                    plain_tiles):
    j = pl.program_id(1)
    tn = o_ref.shape[1]
    chunks = [slice(c0, c0 + EPILOGUE_COLS) for c0 in range(0, tn, EPILOGUE_COLS)]

    @pl.when(j == 0)
    def _():
        xb_ref[...] = x_ref[...].astype(BF16)

    def matmul(cs):
        return jnp.dot(xb_ref[...], w_ref[:, cs], preferred_element_type=F32)

    @pl.when(j < rope_tiles)
    def _():
        scale = jnp.where(j < q_tiles, QK_DIM ** -0.5 * math.log2(math.e), 1.0).astype(F32)
        reps = EPILOGUE_COLS // LANES
        c = jnp.tile(cos_ref[...] * scale, (1, reps))
        s = jnp.tile(sin_ref[...] * scale, (1, reps))
        lane = lax.broadcasted_iota(jnp.int32, c.shape, 1)
        first_half = (lane % QK_DIM) < (QK_DIM // 2)
        for cs in chunks:
            acc = matmul(cs)
            rot = jnp.where(first_half, pltpu.roll(acc, EPILOGUE_COLS - QK_DIM // 2, 1),
                            pltpu.roll(acc, QK_DIM // 2, 1))
            o_ref[:, cs] = (acc * c + rot * s).astype(o_ref.dtype)

    @pl.when((j >= rope_tiles) & (j < rope_tiles + plain_tiles))
    def _():
        for cs in chunks:
            o_ref[:, cs] = matmul(cs).astype(o_ref.dtype)

    @pl.when(j >= rope_tiles + plain_tiles)
    def _():
        for cs in chunks:
            o_ref[:, cs] = jax.nn.sigmoid(matmul(cs)).astype(o_ref.dtype)


def _in_proj(x2, w_bf, layer, cos_t, sin_t, *, q_cols, rope_cols, plain_cols):
    n, d = x2.shape
    cols = w_bf.shape[2]
    tm = min(1024, n)
    tn = min(1024, q_cols)
    kern = functools.partial(_in_proj_kernel, q_tiles=q_cols // tn, rope_tiles=rope_cols // tn,
                             plain_tiles=plain_cols // tn)
    return pl.pallas_call(
        kern,
        grid=(n // tm, cols // tn),
        in_specs=[
            pl.BlockSpec((tm, d), lambda i, j: (i, 0)),
            pl.BlockSpec((None, d, tn), lambda i, j: (layer, 0, j)),
            pl.BlockSpec((tm, LANES), lambda i, j: (i, 0)),
            pl.BlockSpec((tm, LANES), lambda i, j: (i, 0)),
        ],
        out_specs=pl.BlockSpec((tm, tn), lambda i, j: (i, j)),
        out_shape=jax.ShapeDtypeStruct((n, cols), BF16),
        scratch_shapes=[pltpu.VMEM((tm, d), BF16)],
        compiler_params=_params("parallel", "arbitrary"),
        name="in_proj",
    )(x2, w_bf, cos_t, sin_t)


ONES_ROWS = 16


def _attn_kernel(lq_ref, g_ref, q_ref, k_ref, v_ref, o_ref, vt_ref, sa_ref, sb_ref, bma_ref, bmb_ref,
                 m_ref, acc_ref, *, lam_init):
    qi = pl.program_id(2)
    t = q_ref.shape[0]
    seq = k_ref.shape[0]

    @pl.when(qi == 0)
    def _():
        for r in range(seq // t):
            rows = slice(r * t, (r + 1) * t)
            vt_ref[0:V_DIM, rows] = v_ref[rows, :].astype(F32).T.astype(BF16)
        vt_ref[V_DIM:V_DIM + ONES_ROWS, :] = jnp.ones((ONES_ROWS, seq), BF16)

    q = q_ref[...]
    m_ref[...] = jnp.full(m_ref.shape, NEG_BIG, F32)
    acc_ref[...] = jnp.zeros(acc_ref.shape, F32)

    def scores(ki, buf, masked):
        s_ref, bm_ref = buf
        k = k_ref[pl.ds(pl.multiple_of(ki * t, t), t), :]
        if masked:
            kpos = lax.broadcasted_iota(jnp.int32, (t, t), 0)
            qpos = lax.broadcasted_iota(jnp.int32, (t, t), 1)
            keep = kpos <= qpos
        for c in range(2):
            cs = slice(c * QK_DIM, (c + 1) * QK_DIM)
            st = lax.dot_general(k[:, cs], q[:, cs], (((1,), (1,)), ((), ())),
                                 preferred_element_type=F32)
            if masked:
                st = jnp.where(keep, st, NEG_BIG)
            s_ref[c] = st
            bm_ref[c:c + 1, :] = jnp.max(st, axis=0, keepdims=True)

    def softmax_pv(ki, buf):
        s_ref, bm_ref = buf
        vt = vt_ref[:, pl.ds(pl.multiple_of(ki * t, t), t)]
        for c in range(2):
            m_prev = m_ref[c:c + 1, :]
            m_new = jnp.maximum(m_prev, bm_ref[c:c + 1, :])
            alpha = jnp.exp2(m_prev - m_new)
            p = jnp.exp2(s_ref[c] - m_new).astype(BF16)
            acc_ref[c] = alpha * acc_ref[c] + jnp.dot(vt, p, preferred_element_type=F32)
            m_ref[c:c + 1, :] = m_new

    buf_a = (sa_ref, bma_ref)
    buf_b = (sb_ref, bmb_ref)

    @pl.when(qi == 0)
    def _():
        scores(0, buf_a, True)

    @pl.when(qi > 0)
    def _():
        scores(0, buf_a, False)

    def pair(i, carry):
        scores(2 * i + 1, buf_b, False)
        softmax_pv(2 * i, buf_a)
        scores(2 * i + 2, buf_a, False)
        softmax_pv(2 * i + 1, buf_b)
        return carry

    n_pairs = jnp.maximum(qi - 1, 0) // 2
    lax.fori_loop(0, n_pairs, pair, 0)
    done = 2 * n_pairs

    @pl.when(qi - done == 0)
    def _():
        softmax_pv(qi, buf_a)

    @pl.when(qi - done == 1)
    def _():
        scores(qi, buf_b, True)
        softmax_pv(done, buf_a)
        softmax_pv(qi, buf_b)

    @pl.when(qi - done == 2)
    def _():
        scores(done + 1, buf_b, False)
        softmax_pv(done, buf_a)
        scores(qi, buf_a, True)
        softmax_pv(done + 1, buf_b)
        softmax_pv(qi, buf_a)

    lq = lq_ref[...]
    lam = (jnp.exp(jnp.sum(lq[0:1] * lq[1:2], axis=1, keepdims=True))
           - jnp.exp(jnp.sum(lq[2:3] * lq[3:4], axis=1, keepdims=True)) + lam_init)
    o1 = acc_ref[0, 0:V_DIM, :] / acc_ref[0, V_DIM:V_DIM + 1, :]
    o2 = acc_ref[1, 0:V_DIM, :] / acc_ref[1, V_DIM:V_DIM + 1, :]
    out_t = o1 - lam * o2
    out_t = out_t * lax.rsqrt(jnp.mean(out_t * out_t, axis=0, keepdims=True) + RMS_EPS)
    out = out_t.T * g_ref[...] * (1.0 - lam_init)
    o_ref[...] = out.astype(o_ref.dtype)


def _diff_attn(proj, lam_qk, subln_g, *, bsz, seq, heads, lam_init):
    n = proj.shape[0]
    t = min(512, seq)
    nq = seq // t
    kern = functools.partial(_attn_kernel, lam_init=lam_init)
    return pl.pallas_call(
        kern,
        grid=(bsz, heads, nq),
        in_specs=[
            pl.BlockSpec((4, QK_DIM), lambda b, h, qi: (0, 0)),
            pl.BlockSpec((1, V_DIM), lambda b, h, qi: (0, 0)),
            pl.BlockSpec((t, HEAD_COLS), lambda b, h, qi: (b * nq + qi, h)),
            pl.BlockSpec((seq, HEAD_COLS), lambda b, h, qi: (b, heads + h)),
            pl.BlockSpec((seq, V_DIM), lambda b, h, qi: (b, 2 * heads + h)),
        ],
        out_specs=pl.BlockSpec((t, V_DIM), lambda b, h, qi: (b * nq + qi, h)),
        out_shape=jax.ShapeDtypeStruct((n, heads * V_DIM), BF16),
        scratch_shapes=[
            pltpu.VMEM((V_DIM + ONES_ROWS, seq), BF16),
            pltpu.VMEM((2, t, t), F32), pltpu.VMEM((2, t, t), F32),
            pltpu.VMEM((2, t), F32), pltpu.VMEM((2, t), F32),
            pltpu.VMEM((2, t), F32), pltpu.VMEM((2, V_DIM + ONES_ROWS, t), F32),
        ],
        compiler_params=_params("parallel", "parallel", "arbitrary"),
        name="diff_attn",
    )(lam_qk, subln_g.reshape(1, V_DIM), proj, proj, proj)


def _cmul(ar, ai, br, bi):
    return ar * br - ai * bi, ar * bi + ai * br


def _ssm_prep_kernel(are_ref, aim_ref, ldt_ref, bre_ref, bim_ref,
                     bbre_ref, bbim_ref, pre_ref, pim_ref, hre_ref, him_ref):
    a_re = are_ref[...]
    a_im = aim_ref[...]
    dt = jnp.exp(ldt_ref[...])
    mag = jnp.exp(dt * a_re)
    ab_re = mag * jnp.cos(dt * a_im)
    ab_im = mag * jnp.sin(dt * a_im)
    den = a_re * a_re + a_im * a_im
    nr, ni = ab_re - 1.0, ab_im
    z_re = (nr * a_re + ni * a_im) / den
    z_im = (ni * a_re - nr * a_im) / den
    b_re = bre_ref[...]
    b_im = bim_ref[...]
    bbre_ref[...] = z_re * b_re - z_im * b_im
    bbim_ref[...] = z_re * b_im + z_im * b_re
    sub = pre_ref.shape[0]
    pre_ref[0:1, :] = ab_re
    pim_ref[0:1, :] = ab_im
    filled = 1
    while filled < sub:
        tr = pre_ref[filled - 1:filled, :]
        ti = pim_ref[filled - 1:filled, :]
        nr2, ni2 = _cmul(pre_ref[0:filled, :], pim_ref[0:filled, :], tr, ti)
        pre_ref[filled:2 * filled, :] = nr2
        pim_ref[filled:2 * filled, :] = ni2
        filled *= 2
    er = pre_ref[sub - 1:sub, :]
    ei = pim_ref[sub - 1:sub, :]
    for r in range(3):
        hre_ref[r:r + 1, :] = er
        him_ref[r:r + 1, :] = ei
        er, ei = _cmul(er, ei, er, ei)


def _ssm_prep(a_re, a_im, log_dt, b_re, b_im, *, sub):
    g, p = a_re.shape
    w = g * p
    row = lambda t: t.reshape(1, w)
    ldt = jnp.broadcast_to(log_dt[:, None], (g, p))
    tb = lambda t: jnp.transpose(t, (2, 0, 1)).reshape(SSM_GROUP, w)
    full = lambda r: pl.BlockSpec((r, w), lambda: (0, 0))
    shapes = [(SSM_GROUP, w), (SSM_GROUP, w), (sub, w), (sub, w), (3, w), (3, w)]
    return pl.pallas_call(
        _ssm_prep_kernel,
        in_specs=[full(1), full(1), full(1), full(SSM_GROUP), full(SSM_GROUP)],
        out_specs=[full(s[0]) for s in shapes],
        out_shape=[jax.ShapeDtypeStruct(s, F32) for s in shapes],
        compiler_params=pltpu.CompilerParams(vmem_limit_bytes=VMEM_LIMIT),
        name="ssm_prep",
    )(row(a_re), row(a_im), row(ldt), tb(b_re), tb(b_im))


def _block_diag_tiles(blocks, per_tile):
    g, r, c = blocks.shape
    t = blocks.reshape(g // per_tile, per_tile, r, c)
    eye = jnp.eye(per_tile, dtype=blocks.dtype)
    out = t[:, :, :, None, :] * eye[None, :, None, :, None]
    return out.reshape(g // per_tile, per_tile * r, per_tile * c)


GROUPS_PER_TILE = LANES // SSM_GROUP
STATES_PER_TILE = GROUPS_PER_TILE * SSM_STATE
SCAN_STRIP = 512


def _ssm_kernel(u_ref, perm_ref, permt_ref, bb_ref, cre_ref, cim_ref, d_ref, wglu_ref,
                are_ref, aim_ref, pre_ref, pim_ref, hre_ref, him_ref,
                o_ref, xr_ref, xi_ref, xb_ref, hr_ref, hi_ref):
    ci = pl.program_id(1)
    chunk = u_ref.shape[0]
    sub = chunk // SUBLANES
    width = xr_ref.shape[1]
    n_tiles = width // STATES_PER_TILE
    bcast = lambda row: jnp.broadcast_to(row, (SUBLANES, STATES_PER_TILE))

    @pl.when(ci == 0)
    def _():
        hr_ref[...] = jnp.zeros(hr_ref.shape, F32)
        hi_ref[...] = jnp.zeros(hi_ref.shape, F32)

    u = jnp.dot(perm_ref[...], u_ref[...], preferred_element_type=F32)
    ub = u.astype(BF16)
    row8 = lax.broadcasted_iota(jnp.int32, (SUBLANES, STATES_PER_TILE), 0)

    def drive(t):
        bu = jnp.dot(ub[:, t * LANES:(t + 1) * LANES], bb_ref[t], preferred_element_type=F32)
        sl = slice(t * STATES_PER_TILE, (t + 1) * STATES_PER_TILE)
        xr_ref[:, sl] = bu[:, :STATES_PER_TILE]
        xi_ref[:, sl] = bu[:, STATES_PER_TILE:]

    def scan(t):
        sl = slice(t * STATES_PER_TILE, (t + 1) * STATES_PER_TILE)
        sli = slice(width + sl.start, width + sl.stop)
        ar = bcast(are_ref[:, sl])
        ai = bcast(aim_ref[:, sl])
        xr = jnp.where(row8 == 0, bcast(hr_ref[:, sl]), 0.0)
        xi = jnp.where(row8 == 0, bcast(hi_ref[:, sl]), 0.0)
        for j in range(sub):
            rows = slice(j * SUBLANES, (j + 1) * SUBLANES)
            xr, xi = (ar * xr - ai * xi + xr_ref[rows, sl], ar * xi + ai * xr + xi_ref[rows, sl])
            xr_ref[rows, sl] = xr
            xi_ref[rows, sl] = xi
        er, ei = xr, xi
        for r, d in enumerate((1, 2, 4)):
            pr = bcast(hre_ref[r:r + 1, sl])
            pi = bcast(him_ref[r:r + 1, sl])
            sr = jnp.where(row8 >= d, pltpu.roll(er, d, 0), 0.0)
            si = jnp.where(row8 >= d, pltpu.roll(ei, d, 0), 0.0)
            er, ei = er + pr * sr - pi * si, ei + pr * si + pi * sr
        hr_ref[:, sl] = er[SUBLANES - 1:SUBLANES, :]
        hi_ref[:, sl] = ei[SUBLANES - 1:SUBLANES, :]
        cr = jnp.where(row8 >= 1, pltpu.roll(er, 1, 0), 0.0)
        cim = jnp.where(row8 >= 1, pltpu.roll(ei, 1, 0), 0.0)
        for j in range(0, sub, 2):
            halves_r, halves_i = [], []
            for jj in (j, j + 1):
                rows = slice(jj * SUBLANES, (jj + 1) * SUBLANES)
                pr = bcast(pre_ref[jj:jj + 1, sl])
                pi = bcast(pim_ref[jj:jj + 1, sl])
                halves_r.append(xr_ref[rows, sl] + pr * cr - pi * cim)
                halves_i.append(xi_ref[rows, sl] + pr * cim + pi * cr)
            rows2 = slice(j * SUBLANES, (j + 2) * SUBLANES)
            xb_ref[rows2, sl] = jnp.concatenate(halves_r, axis=0).astype(BF16)
            xb_ref[rows2, sli] = jnp.concatenate(halves_i, axis=0).astype(BF16)

    def readout(t):
        sl = slice(t * STATES_PER_TILE, (t + 1) * STATES_PER_TILE)
        sli = slice(width + sl.start, width + sl.stop)
        return (jnp.dot(xb_ref[:, sl], cre_ref[t], preferred_element_type=F32)
                - jnp.dot(xb_ref[:, sli], cim_ref[t], preferred_element_type=F32))

    ys = []
    drive(0)
    for t in range(n_tiles):
        if t + 1 < n_tiles:
            drive(t + 1)
        scan(t)
        ys.append(readout(t))
    y = jnp.concatenate(ys, axis=1) + d_ref[...] * u
    g = jax.nn.gelu(y)
    z = jnp.dot(g.astype(BF16), wglu_ref[...], preferred_element_type=F32)
    out = (g * jax.nn.sigmoid(z)).astype(BF16)
    o_ref[...] = jnp.dot(permt_ref[...], out, preferred_element_type=F32).astype(o_ref.dtype)


def _s5_ssm(proj, u_col_block, prep, c_re, c_im, d_skip, w_glu_bf, layer, *, bsz, seq, chunk):
    bb_re, bb_im, pow_re, pow_im, hs_re, hs_im = prep
    ab_re, ab_im = pow_re[0:1], pow_im[0:1]
    n = proj.shape[0]
    groups = c_re.shape[0]
    ssm_w = groups * SSM_GROUP
    width = groups * SSM_STATE
    n_tiles = ssm_w // LANES
    sub = chunk // SUBLANES
    nc = seq // chunk
    to_blocks = lambda t: jnp.transpose(t.reshape(SSM_GROUP, groups, SSM_STATE), (1, 0, 2))
    bb = jnp.concatenate([_block_diag_tiles(to_blocks(bb_re), GROUPS_PER_TILE),
                          _block_diag_tiles(to_blocks(bb_im), GROUPS_PER_TILE)], axis=2).astype(BF16)
    ct = lambda t: _block_diag_tiles(jnp.transpose(t, (0, 2, 1)), GROUPS_PER_TILE).astype(BF16)
    r = jnp.arange(chunk)
    src = (r % SUBLANES) * sub + r // SUBLANES
    perm = (src[:, None] == jnp.arange(chunk)[None, :]).astype(BF16)
    const2 = lambda shape: pl.BlockSpec(shape, lambda b, c: (0, 0))
    const3 = lambda shape: pl.BlockSpec(shape, lambda b, c: (0, 0, 0))
    return pl.pallas_call(
        _ssm_kernel,
        grid=(bsz, nc),
        in_specs=[
            pl.BlockSpec((chunk, ssm_w), lambda b, c: (b * nc + c, u_col_block)),
            const2((chunk, chunk)), const2((chunk, chunk)),
            const3((n_tiles, LANES, 2 * STATES_PER_TILE)),
            const3((n_tiles, STATES_PER_TILE, LANES)), const3((n_tiles, STATES_PER_TILE, LANES)),
            const2((1, ssm_w)), pl.BlockSpec((None, ssm_w, ssm_w), lambda b, c: (layer, 0, 0)),
            const2((1, width)), const2((1, width)),
            const2((sub, width)), const2((sub, width)),
            const2((3, width)), const2((3, width)),
        ],
        out_specs=pl.BlockSpec((chunk, ssm_w), lambda b, c: (b * nc + c, 0)),
        out_shape=jax.ShapeDtypeStruct((n, ssm_w), BF16),
        scratch_shapes=[
            pltpu.VMEM((chunk, width), F32), pltpu.VMEM((chunk, width), F32),
            pltpu.VMEM((chunk, 2 * width), BF16),
            pltpu.VMEM((1, width), F32), pltpu.VMEM((1, width), F32),
        ],
        compiler_params=_params("parallel", "arbitrary"),
        name="s5_ssm",
    )(proj, perm, perm.T, bb, ct(c_re), ct(c_im), d_skip.reshape(1, ssm_w), w_glu_bf,
      ab_re, ab_im, pow_re, pow_im, hs_re, hs_im)


def _layer_norm(y, g, b):
    mu = jnp.mean(y, axis=1, keepdims=True)
    yc = y - mu
    var = jnp.mean(yc * yc, axis=1, keepdims=True)
    return yc * lax.rsqrt(var + LN_EPS) * g + b


def _mix_kernel(x_ref, attn_ref, ssm_ref, ga_ref, gs_ref, wa_ref, ws_ref, wo_ref, g_ref, b_ref,
                o_ref, *, alpha):
    a = jnp.dot(attn_ref[...], wa_ref[...], preferred_element_type=F32)
    s = jnp.dot(ssm_ref[...], ws_ref[...], preferred_element_type=F32)
    merged = ga_ref[...].astype(F32) * a + gs_ref[...].astype(F32) * s
    mix = jnp.dot(merged.astype(BF16), wo_ref[...], preferred_element_type=F32)
    o_ref[...] = _layer_norm(alpha * x_ref[...] + mix, g_ref[...], b_ref[...])


def _mix_out(x2, attn, ssm, proj, ga_block, wa, ws, wo, layer, ln_g, ln_b, *, alpha):
    n, d = x2.shape
    aw = attn.shape[1]
    sw = ssm.shape[1]
    tm = min(256, n)
    row = lambda w: pl.BlockSpec((tm, w), lambda i: (i, 0))
    const = lambda shape: pl.BlockSpec(shape, lambda i: (0, 0), pipeline_mode=pl.Buffered(1))
    wconst = lambda shape: pl.BlockSpec((None,) + shape, lambda i: (layer, 0, 0),
                                        pipeline_mode=pl.Buffered(1))
    return pl.pallas_call(
        functools.partial(_mix_kernel, alpha=alpha),
        grid=(n // tm,),
        in_specs=[
            row(d), row(aw), row(sw),
            pl.BlockSpec((tm, d), lambda i: (i, ga_block)),
            pl.BlockSpec((tm, d), lambda i: (i, ga_block + 1)),
            wconst((aw, d)), wconst((sw, d)), wconst((d, d)), const((1, d)), const((1, d)),
        ],
        out_specs=row(d),
        out_shape=jax.ShapeDtypeStruct((n, d), F32),
        compiler_params=_params("parallel"),
        name="mix_out",
    )(x2, attn, ssm, proj, proj, wa, ws, wo, ln_g.reshape(1, d), ln_b.reshape(1, d))


def _mlp_kernel(x_ref, w1_ref, w2_ref, g_ref, b_ref, o_ref, xb_ref, acc_ref, *, alpha):
    f = pl.program_id(1)

    @pl.when(f == 0)
    def _():
        xb_ref[...] = x_ref[...].astype(BF16)
        acc_ref[...] = jnp.zeros(acc_ref.shape, F32)

    h = jnp.dot(xb_ref[...], w1_ref[...], preferred_element_type=F32)
    h = jnp.square(jnp.maximum(h, 0.0))
    acc_ref[...] += jnp.dot(h.astype(BF16), w2_ref[...], preferred_element_type=F32)

    @pl.when(f == pl.num_programs(1) - 1)
    def _():
        o_ref[...] = _layer_norm(alpha * x_ref[...] + acc_ref[...], g_ref[...], b_ref[...])


def _mlp(x2, w1, w2, layer, ln_g, ln_b, *, alpha):
    n, d = x2.shape
    dff = w1.shape[2]
    tm = min(512, n)
    tf = min(1024, dff)
    return pl.pallas_call(
        functools.partial(_mlp_kernel, alpha=alpha),
        grid=(n // tm, dff // tf),
        in_specs=[
            pl.BlockSpec((tm, d), lambda i, f: (i, 0)),
            pl.BlockSpec((None, d, tf), lambda i, f: (layer, 0, f)),
            pl.BlockSpec((None, tf, d), lambda i, f: (layer, f, 0)),
            pl.BlockSpec((1, d), lambda i, f: (0, 0)),
            pl.BlockSpec((1, d), lambda i, f: (0, 0)),
        ],
        out_specs=pl.BlockSpec((tm, d), lambda i, f: (i, 0)),
        out_shape=jax.ShapeDtypeStruct((n, d), F32),
        scratch_shapes=[pltpu.VMEM((tm, d), BF16), pltpu.VMEM((tm, d), F32)],
        compiler_params=_params("parallel", "arbitrary"),
        name="mlp",
    )(x2, w1, w2, ln_g.reshape(1, d), ln_b.reshape(1, d))


def _rope_tables(positions):
    inv = ROPE_THETA ** (-jnp.arange(0, QK_DIM, 2, dtype=F32) / QK_DIM)
    ang = positions.astype(F32).reshape(-1, 1) * inv
    cos = jnp.cos(ang)
    sin = jnp.sin(ang)
    reps = LANES // QK_DIM
    return (jnp.tile(jnp.concatenate([cos, cos], axis=1), (1, reps)),
            jnp.tile(jnp.concatenate([-sin, sin], axis=1), (1, reps)))


def kernel(x, positions, w_in, lambda_qk, subln_g, ssm_a_re, ssm_a_im, ssm_log_dt, ssm_b_re,
           ssm_b_im, ssm_c_re, ssm_c_im, ssm_d, w_glu, w_attn_up, w_ssm_up, w_out, ln1_g, ln1_b,
           ln2_g, ln2_b, w_mlp_up, w_mlp_down):
    bsz, seq, d = x.shape
    depth = w_in.shape[0]
    attn_w = w_attn_up.shape[1]
    ssm_w = w_ssm_up.shape[1]
    heads = attn_w // V_DIM
    q_cols = heads * HEAD_COLS
    assert w_in.shape[2] == 2 * q_cols + attn_w + ssm_w + 2 * d
    assert q_cols == attn_w == ssm_w and d % ssm_w == 0
    alpha = (2.0 * depth) ** 0.25
    chunk = min(512, seq)

    cos_t, sin_t = _rope_tables(positions)
    w_in_bf, w_glu_bf, w_attn_up_bf, w_ssm_up_bf, w_out_bf, w_mlp_up_bf, w_mlp_down_bf = (
        w.astype(BF16) for w in (w_in, w_glu, w_attn_up, w_ssm_up, w_out, w_mlp_up, w_mlp_down))
    x2 = x.reshape(bsz * seq, d)
    for l in range(depth):
        lam_init = _lambda_init(l)
        proj = _in_proj(x2, w_in_bf, l, cos_t, sin_t, q_cols=q_cols,
                        rope_cols=2 * q_cols, plain_cols=attn_w + ssm_w)
        attn = _diff_attn(proj, lambda_qk[l], subln_g[l], bsz=bsz, seq=seq, heads=heads,
                          lam_init=lam_init)
        prep = _ssm_prep(ssm_a_re[l], ssm_a_im[l], ssm_log_dt[l], ssm_b_re[l], ssm_b_im[l],
                         sub=chunk // SUBLANES)
        ssm = _s5_ssm(proj, (2 * q_cols + attn_w) // ssm_w, prep, ssm_c_re[l], ssm_c_im[l],
                      ssm_d[l], w_glu_bf, l, bsz=bsz, seq=seq, chunk=chunk)
        x2 = _mix_out(x2, attn, ssm, proj, (2 * q_cols + attn_w + ssm_w) // d,
                      w_attn_up_bf, w_ssm_up_bf, w_out_bf, l, ln1_g[l], ln1_b[l], alpha=alpha)
        x2 = _mlp(x2, w_mlp_up_bf, w_mlp_down_bf, l, ln2_g[l], ln2_b[l], alpha=alpha)
    return x2.reshape(bsz, seq, d)
```

```python
import functools
import math

import jax
import jax.numpy as jnp
from jax import lax
from jax.experimental import pallas as pl
from jax.experimental.pallas import tpu as pltpu

F32 = jnp.float32
BF16 = jnp.bfloat16

QK_DIM = 64
V_DIM = 2 * QK_DIM
HEAD_COLS = 2 * QK_DIM
SSM_GROUP = 16
SSM_STATE = 64
ROPE_THETA = 10000.0
LN_EPS = 1e-5
RMS_EPS = 1e-5
LANES = 128
SUBLANES = 8
VMEM_LIMIT = 56 * 1024 * 1024
NEG_BIG = -1e30


def _lambda_init(layer):
    return 0.8 - 0.6 * math.exp(-0.3 * layer)


def _params(*sem):
    return pltpu.CompilerParams(dimension_semantics=sem, vmem_limit_bytes=VMEM_LIMIT)


EPILOGUE_COLS = 256


def _in_proj_kernel(x_ref, w_ref, cos_ref, sin_ref, o_ref, xb_ref, *, q_tiles, rope_tiles,
                    plain_tiles):
    j = pl.program_id(1)
    tn = o_ref.shape[1]
    chunks = [slice(c0, c0 + EPILOGUE_COLS) for c0 in range(0, tn, EPILOGUE_COLS)]

    @pl.when(j == 0)
    def _():
        xb_ref[...] = x_ref[...].astype(BF16)

    def matmul(cs):
        return jnp.dot(xb_ref[...], w_ref[:, cs], preferred_element_type=F32)

    @pl.when(j < rope_tiles)
    def _():
        scale = jnp.where(j < q_tiles, QK_DIM ** -0.5 * math.log2(math.e), 1.0).astype(F32)
        reps = EPILOGUE_COLS // LANES
        c = jnp.tile(cos_ref[...] * scale, (1, reps))
        s = jnp.tile(sin_ref[...] * scale, (1, reps))
        lane = lax.broadcasted_iota(jnp.int32, c.shape, 1)
        first_half = (lane % QK_DIM) < (QK_DIM // 2)
        for cs in chunks:
            acc = matmul(cs)
            rot = jnp.where(first_half, pltpu.roll(acc, EPILOGUE_COLS - QK_DIM // 2, 1),
                            pltpu.roll(acc, QK_DIM // 2, 1))
            o_ref[:, cs] = (acc * c + rot * s).astype(o_ref.dtype)

    @pl.when((j >= rope_tiles) & (j < rope_tiles + plain_tiles))
    def _():
        for cs in chunks:
            o_ref[:, cs] = matmul(cs).astype(o_ref.dtype)

    @pl.when(j >= rope_tiles + plain_tiles)
    def _():
        for cs in chunks:
            o_ref[:, cs] = jax.nn.sigmoid(matmul(cs)).astype(o_ref.dtype)


def _in_proj(x2, w_bf, layer, cos_t, sin_t, *, q_cols, rope_cols, plain_cols):
    n, d = x2.shape
    cols = w_bf.shape[2]
    tm = min(1024, n)
    tn = min(1024, q_cols)
    kern = functools.partial(_in_proj_kernel, q_tiles=q_cols // tn, rope_tiles=rope_cols // tn,
                             plain_tiles=plain_cols // tn)
    return pl.pallas_call(
        kern,
        grid=(n // tm, cols // tn),
        in_specs=[
            pl.BlockSpec((tm, d), lambda i, j: (i, 0)),
            pl.BlockSpec((None, d, tn), lambda i, j: (layer, 0, j)),
            pl.BlockSpec((tm, LANES), lambda i, j: (i, 0)),
            pl.BlockSpec((tm, LANES), lambda i, j: (i, 0)),
        ],
        out_specs=pl.BlockSpec((tm, tn), lambda i, j: (i, j)),
        out_shape=jax.ShapeDtypeStruct((n, cols), BF16),
        scratch_shapes=[pltpu.VMEM((tm, d), BF16)],
        compiler_params=_params("parallel", "arbitrary"),
        name="in_proj",
    )(x2, w_bf, cos_t, sin_t)


ONES_ROWS = 16


def _attn_kernel(lq_ref, g_ref, q_ref, k_ref, v_ref, o_ref, vt_ref, qt_ref, sa_ref, sb_ref, bma_ref, bmb_ref,
                 m_ref, acc_ref, *, lam_init):
    qi = pl.program_id(2)
    t = q_ref.shape[0]
    seq = k_ref.shape[0]

    @pl.when(qi == 0)
    def _():
        for r in range(seq // t):
            rows = slice(r * t, (r + 1) * t)
            vt_ref[0:V_DIM, rows] = v_ref[rows, :].astype(F32).T.astype(BF16)
        vt_ref[V_DIM:V_DIM + ONES_ROWS, :] = jnp.ones((ONES_ROWS, seq), BF16)

    qt = q_ref[...].astype(F32).T
    comp = lax.broadcasted_iota(jnp.int32, qt.shape, 0) // QK_DIM
    for c in range(2):
        qt_ref[c] = jnp.where(comp == c, qt, 0.0).astype(BF16)
    m_ref[...] = jnp.full(m_ref.shape, NEG_BIG, F32)
    acc_ref[...] = jnp.zeros(acc_ref.shape, F32)

    def scores(ki, buf, masked):
        s_ref, bm_ref = buf
        k = k_ref[pl.ds(pl.multiple_of(ki * t, t), t), :]
        if masked:
            kpos = lax.broadcasted_iota(jnp.int32, (t, t), 0)
            qpos = lax.broadcasted_iota(jnp.int32, (t, t), 1)
            keep = kpos <= qpos
        for c in range(2):
            st = jnp.dot(k, qt_ref[c], preferred_element_type=F32)
            if masked:
                st = jnp.where(keep, st, NEG_BIG)
            s_ref[c] = st
            bm_ref[c:c + 1, :] = jnp.max(st, axis=0, keepdims=True)

    def softmax_pv(ki, buf):
        s_ref, bm_ref = buf
        vt = vt_ref[:, pl.ds(pl.multiple_of(ki * t, t), t)]
        for c in range(2):
            m_prev = m_ref[c:c + 1, :]
            m_new = jnp.maximum(m_prev, bm_ref[c:c + 1, :])
            alpha = jnp.exp2(m_prev - m_new)
            p = jnp.exp2(s_ref[c] - m_new).astype(BF16)
            acc_ref[c] = alpha * acc_ref[c] + jnp.dot(vt, p, preferred_element_type=F32)
            m_ref[c:c + 1, :] = m_new

    buf_a = (sa_ref, bma_ref)
    buf_b = (sb_ref, bmb_ref)

    @pl.when(qi == 0)
    def _():
        scores(0, buf_a, True)

    @pl.when(qi > 0)
    def _():
        scores(0, buf_a, False)

    def pair(i, carry):
        scores(2 * i + 1, buf_b, False)
        softmax_pv(2 * i, buf_a)
        scores(2 * i + 2, buf_a, False)
        softmax_pv(2 * i + 1, buf_b)
        return carry

    n_pairs = jnp.maximum(qi - 1, 0) // 2
    lax.fori_loop(0, n_pairs, pair, 0)
    done = 2 * n_pairs

    @pl.when(qi - done == 0)
    def _():
        softmax_pv(qi, buf_a)

    @pl.when(qi - done == 1)
    def _():
        scores(qi, buf_b, True)
        softmax_pv(done, buf_a)
        softmax_pv(qi, buf_b)

    @pl.when(qi - done == 2)
    def _():
        scores(done + 1, buf_b, False)
        softmax_pv(done, buf_a)
        scores(qi, buf_a, True)
        softmax_pv(done + 1, buf_b)
        softmax_pv(qi, buf_a)

    lq = lq_ref[...]
    lam = (jnp.exp(jnp.sum(lq[0:1] * lq[1:2], axis=1, keepdims=True))
           - jnp.exp(jnp.sum(lq[2:3] * lq[3:4], axis=1, keepdims=True)) + lam_init)
    o1 = acc_ref[0, 0:V_DIM, :] / acc_ref[0, V_DIM:V_DIM + 1, :]
    o2 = acc_ref[1, 0:V_DIM, :] / acc_ref[1, V_DIM:V_DIM + 1, :]
    out_t = o1 - lam * o2
    out_t = out_t * lax.rsqrt(jnp.mean(out_t * out_t, axis=0, keepdims=True) + RMS_EPS)
    out = out_t.T * g_ref[...] * (1.0 - lam_init)
    o_ref[...] = out.astype(o_ref.dtype)


def _diff_attn(proj, lam_qk, subln_g, *, bsz, seq, heads, lam_init):
    n = proj.shape[0]
    t = min(512, seq)
    nq = seq // t
    kern = functools.partial(_attn_kernel, lam_init=lam_init)
    return pl.pallas_call(
        kern,
        grid=(bsz, heads, nq),
        in_specs=[
            pl.BlockSpec((4, QK_DIM), lambda b, h, qi: (0, 0)),
            pl.BlockSpec((1, V_DIM), lambda b, h, qi: (0, 0)),
            pl.BlockSpec((t, HEAD_COLS), lambda b, h, qi: (b * nq + qi, h)),
            pl.BlockSpec((seq, HEAD_COLS), lambda b, h, qi: (b, heads + h)),
            pl.BlockSpec((seq, V_DIM), lambda b, h, qi: (b, 2 * heads + h)),
        ],
        out_specs=pl.BlockSpec((t, V_DIM), lambda b, h, qi: (b * nq + qi, h)),
        out_shape=jax.ShapeDtypeStruct((n, heads * V_DIM), BF16),
        scratch_shapes=[
            pltpu.VMEM((V_DIM + ONES_ROWS, seq), BF16),
            pltpu.VMEM((2, HEAD_COLS, t), BF16),
            pltpu.VMEM((2, t, t), F32), pltpu.VMEM((2, t, t), F32),
            pltpu.VMEM((2, t), F32), pltpu.VMEM((2, t), F32),
            pltpu.VMEM((2, t), F32), pltpu.VMEM((2, V_DIM + ONES_ROWS, t), F32),
        ],
        compiler_params=_params("parallel", "parallel", "arbitrary"),
        name="diff_attn",
    )(lam_qk, subln_g.reshape(1, V_DIM), proj, proj, proj)


def _cmul(ar, ai, br, bi):
    return ar * br - ai * bi, ar * bi + ai * br


def _ssm_prep_kernel(are_ref, aim_ref, ldt_ref, bre_ref, bim_ref,
                     bbre_ref, bbim_ref, hre_ref, him_ref, *, log2_sub):
    a_re = are_ref[...]
    a_im = aim_ref[...]
    dt = jnp.exp(ldt_ref[...])
    mag = jnp.exp(dt * a_re)
    ab_re = mag * jnp.cos(dt * a_im)
    ab_im = mag * jnp.sin(dt * a_im)
    den = a_re * a_re + a_im * a_im
    nr, ni = ab_re - 1.0, ab_im
    z_re = (nr * a_re + ni * a_im) / den
    z_im = (ni * a_re - nr * a_im) / den
    b_re = bre_ref[...]
    b_im = bim_ref[...]
    bbre_ref[...] = z_re * b_re - z_im * b_im
    bbim_ref[...] = z_re * b_im + z_im * b_re
    hre_ref[0:1, :] = ab_re
    him_ref[0:1, :] = ab_im
    er, ei = ab_re, ab_im
    for _ in range(log2_sub):
        er, ei = _cmul(er, ei, er, ei)
    for r in range(1, 4):
        hre_ref[r:r + 1, :] = er
        him_ref[r:r + 1, :] = ei
        er, ei = _cmul(er, ei, er, ei)


def _ssm_prep(a_re, a_im, log_dt, b_re, b_im, *, sub):
    g, p = a_re.shape
    w = g * p
    log2_sub = sub.bit_length() - 1
    assert sub == 1 << log2_sub
    row = lambda t: t.reshape(1, w)
    ldt = jnp.broadcast_to(log_dt[:, None], (g, p))
    tb = lambda t: jnp.transpose(t, (2, 0, 1)).reshape(SSM_GROUP, w)
    full = lambda r: pl.BlockSpec((r, w), lambda: (0, 0))
    shapes = [(SSM_GROUP, w), (SSM_GROUP, w), (4, w), (4, w)]
    return pl.pallas_call(
        functools.partial(_ssm_prep_kernel, log2_sub=log2_sub),
        in_specs=[full(1), full(1), full(1), full(SSM_GROUP), full(SSM_GROUP)],
        out_specs=[full(s[0]) for s in shapes],
        out_shape=[jax.ShapeDtypeStruct(s, F32) for s in shapes],
        compiler_params=pltpu.CompilerParams(vmem_limit_bytes=VMEM_LIMIT),
        name="ssm_prep",
    )(row(a_re), row(a_im), row(ldt), tb(b_re), tb(b_im))


def _block_diag_tiles(blocks, per_tile):
    g, r, c = blocks.shape
    t = blocks.reshape(g // per_tile, per_tile, r, c)
    eye = jnp.eye(per_tile, dtype=blocks.dtype)
    out = t[:, :, :, None, :] * eye[None, :, None, :, None]
    return out.reshape(g // per_tile, per_tile * r, per_tile * c)


GROUPS_PER_TILE = LANES // SSM_GROUP
STATES_PER_TILE = GROUPS_PER_TILE * SSM_STATE


def _ssm_kernel(u_ref, perm_ref, permt_ref, bb_ref, cre_ref, cim_ref, d_ref, wglu_ref,
                hre_ref, him_ref, o_ref, xr_ref, xi_ref, xb_ref, hr_ref, hi_ref):
    ci = pl.program_id(1)
    chunk = u_ref.shape[0]
    sub = chunk // SUBLANES
    width = xr_ref.shape[1]
    n_tiles = width // STATES_PER_TILE
    bcast = lambda row: jnp.broadcast_to(row, (SUBLANES, STATES_PER_TILE))

    @pl.when(ci == 0)
    def _():
        hr_ref[...] = jnp.zeros(hr_ref.shape, F32)
        hi_ref[...] = jnp.zeros(hi_ref.shape, F32)

    u = jnp.dot(perm_ref[...], u_ref[...], preferred_element_type=F32)
    ub = u.astype(BF16)
    row8 = lax.broadcasted_iota(jnp.int32, (SUBLANES, STATES_PER_TILE), 0)

    def drive(t):
        bu = jnp.dot(ub[:, t * LANES:(t + 1) * LANES], bb_ref[t], preferred_element_type=F32)
        sl = slice(t * STATES_PER_TILE, (t + 1) * STATES_PER_TILE)
        xr_ref[:, sl] = bu[:, :STATES_PER_TILE]
        xi_ref[:, sl] = bu[:, STATES_PER_TILE:]

    def scan(t):
        sl = slice(t * STATES_PER_TILE, (t + 1) * STATES_PER_TILE)
        sli = slice(width + sl.start, width + sl.stop)
        ar = bcast(hre_ref[0:1, sl])
        ai = bcast(him_ref[0:1, sl])
        xr = jnp.where(row8 == 0, bcast(hr_ref[:, sl]), 0.0)
        xi = jnp.where(row8 == 0, bcast(hi_ref[:, sl]), 0.0)
        for j in range(sub):
            rows = slice(j * SUBLANES, (j + 1) * SUBLANES)
            xr, xi = (ar * xr - ai * xi + xr_ref[rows, sl], ar * xi + ai * xr + xi_ref[rows, sl])
            xr_ref[rows, sl] = xr
            xi_ref[rows, sl] = xi
        er, ei = xr, xi
        for r, d in enumerate((1, 2, 4), start=1):
            pr = bcast(hre_ref[r:r + 1, sl])
            pi = bcast(him_ref[r:r + 1, sl])
            sr = jnp.where(row8 >= d, pltpu.roll(er, d, 0), 0.0)
            si = jnp.where(row8 >= d, pltpu.roll(ei, d, 0), 0.0)
            er, ei = er + pr * sr - pi * si, ei + pr * si + pi * sr
        hr_ref[:, sl] = er[SUBLANES - 1:SUBLANES, :]
        hi_ref[:, sl] = ei[SUBLANES - 1:SUBLANES, :]
        cr = jnp.where(row8 >= 1, pltpu.roll(er, 1, 0), 0.0)
        cim = jnp.where(row8 >= 1, pltpu.roll(ei, 1, 0), 0.0)
        for j in range(0, sub, 2):
            halves_r, halves_i = [], []
            for jj in (j, j + 1):
                rows = slice(jj * SUBLANES, (jj + 1) * SUBLANES)
                cr, cim = ar * cr - ai * cim, ar * cim + ai * cr
                halves_r.append(xr_ref[rows, sl] + cr)
                halves_i.append(xi_ref[rows, sl] + cim)
            rows2 = slice(j * SUBLANES, (j + 2) * SUBLANES)
            xb_ref[rows2, sl] = jnp.concatenate(halves_r, axis=0).astype(BF16)
            xb_ref[rows2, sli] = jnp.concatenate(halves_i, axis=0).astype(BF16)

    def readout(t):
        sl = slice(t * STATES_PER_TILE, (t + 1) * STATES_PER_TILE)
        sli = slice(width + sl.start, width + sl.stop)
        return (jnp.dot(xb_ref[:, sl], cre_ref[t], preferred_element_type=F32)
                - jnp.dot(xb_ref[:, sli], cim_ref[t], preferred_element_type=F32))

    ys = []
    drive(0)
    for t in range(n_tiles):
        if t + 1 < n_tiles:
            drive(t + 1)
        scan(t)
        ys.append(readout(t))
    y = jnp.concatenate(ys, axis=1) + d_ref[...] * u
    g = jax.nn.gelu(y)
    z = jnp.dot(g.astype(BF16), wglu_ref[...], preferred_element_type=F32)
    out = (g * jax.nn.sigmoid(z)).astype(BF16)
    o_ref[...] = jnp.dot(permt_ref[...], out, preferred_element_type=F32).astype(o_ref.dtype)


def _s5_ssm(proj, u_col_block, prep, c_re, c_im, d_skip, w_glu_bf, layer, *, bsz, seq, chunk):
    bb_re, bb_im, hs_re, hs_im = prep
    n = proj.shape[0]
    groups = c_re.shape[0]
    ssm_w = groups * SSM_GROUP
    width = groups * SSM_STATE
    n_tiles = ssm_w // LANES
    sub = chunk // SUBLANES
    nc = seq // chunk
    to_blocks = lambda t: jnp.transpose(t.reshape(SSM_GROUP, groups, SSM_STATE), (1, 0, 2))
    bb = jnp.concatenate([_block_diag_tiles(to_blocks(bb_re), GROUPS_PER_TILE),
                          _block_diag_tiles(to_blocks(bb_im), GROUPS_PER_TILE)], axis=2).astype(BF16)
    ct = lambda t: _block_diag_tiles(jnp.transpose(t, (0, 2, 1)), GROUPS_PER_TILE).astype(BF16)
    r = jnp.arange(chunk)
    src = (r % SUBLANES) * sub + r // SUBLANES
    perm = (src[:, None] == jnp.arange(chunk)[None, :]).astype(BF16)
    const2 = lambda shape: pl.BlockSpec(shape, lambda b, c: (0, 0))
    const3 = lambda shape: pl.BlockSpec(shape, lambda b, c: (0, 0, 0))
    return pl.pallas_call(
        _ssm_kernel,
        grid=(bsz, nc),
        in_specs=[
            pl.BlockSpec((chunk, ssm_w), lambda b, c: (b * nc + c, u_col_block)),
            const2((chunk, chunk)), const2((chunk, chunk)),
            const3((n_tiles, LANES, 2 * STATES_PER_TILE)),
            const3((n_tiles, STATES_PER_TILE, LANES)), const3((n_tiles, STATES_PER_TILE, LANES)),
            const2((1, ssm_w)), pl.BlockSpec((None, ssm_w, ssm_w), lambda b, c: (layer, 0, 0)),
            const2((4, width)), const2((4, width)),
        ],
        out_specs=pl.BlockSpec((chunk, ssm_w), lambda b, c: (b * nc + c, 0)),
        out_shape=jax.ShapeDtypeStruct((n, ssm_w), BF16),
        scratch_shapes=[
            pltpu.VMEM((chunk, width), F32), pltpu.VMEM((chunk, width), F32),
            pltpu.VMEM((chunk, 2 * width), BF16),
            pltpu.VMEM((1, width), F32), pltpu.VMEM((1, width), F32),
        ],
        compiler_params=_params("parallel", "arbitrary"),
        name="s5_ssm",
    )(proj, perm, perm.T, bb, ct(c_re), ct(c_im), d_skip.reshape(1, ssm_w), w_glu_bf, hs_re, hs_im)


def _layer_norm(y, g, b):
    mu = jnp.mean(y, axis=1, keepdims=True)
    yc = y - mu
    var = jnp.mean(yc * yc, axis=1, keepdims=True)
    return yc * lax.rsqrt(var + LN_EPS) * g + b


def _mix_kernel(x_ref, attn_ref, ssm_ref, ga_ref, gs_ref, wa_ref, ws_ref, wo_ref, g_ref, b_ref,
                o_ref, *, alpha):
    a = jnp.dot(attn_ref[...], wa_ref[...], preferred_element_type=F32)
    s = jnp.dot(ssm_ref[...], ws_ref[...], preferred_element_type=F32)
    merged = ga_ref[...].astype(F32) * a + gs_ref[...].astype(F32) * s
    mix = jnp.dot(merged.astype(BF16), wo_ref[...], preferred_element_type=F32)
    o_ref[...] = _layer_norm(alpha * x_ref[...] + mix, g_ref[...], b_ref[...])


def _mix_out(x2, attn, ssm, proj, ga_block, wa, ws, wo, layer, ln_g, ln_b, *, alpha):
    n, d = x2.shape
    aw = attn.shape[1]
    sw = ssm.shape[1]
    tm = min(256, n)
    row = lambda w: pl.BlockSpec((tm, w), lambda i: (i, 0))
    const = lambda shape: pl.BlockSpec(shape, lambda i: (0, 0), pipeline_mode=pl.Buffered(1))
    wconst = lambda shape: pl.BlockSpec((None,) + shape, lambda i: (layer, 0, 0),
                                        pipeline_mode=pl.Buffered(1))
    return pl.pallas_call(
        functools.partial(_mix_kernel, alpha=alpha),
        grid=(n // tm,),
        in_specs=[
            row(d), row(aw), row(sw),
            pl.BlockSpec((tm, d), lambda i: (i, ga_block)),
            pl.BlockSpec((tm, d), lambda i: (i, ga_block + 1)),
            wconst((aw, d)), wconst((sw, d)), wconst((d, d)), const((1, d)), const((1, d)),
        ],
        out_specs=row(d),
        out_shape=jax.ShapeDtypeStruct((n, d), F32),
        compiler_params=_params("parallel"),
        name="mix_out",
    )(x2, attn, ssm, proj, proj, wa, ws, wo, ln_g.reshape(1, d), ln_b.reshape(1, d))


def _mlp_kernel(x_ref, w1_ref, w2_ref, g_ref, b_ref, o_ref, xb_ref, acc_ref, *, alpha):
    f = pl.program_id(1)

    @pl.when(f == 0)
    def _():
        xb_ref[...] = x_ref[...].astype(BF16)
        acc_ref[...] = jnp.zeros(acc_ref.shape, F32)

    h = jnp.dot(xb_ref[...], w1_ref[...], preferred_element_type=F32)
    h = jnp.square(jnp.maximum(h, 0.0))
    acc_ref[...] += jnp.dot(h.astype(BF16), w2_ref[...], preferred_element_type=F32)

    @pl.when(f == pl.num_programs(1) - 1)
    def _():
        o_ref[...] = _layer_norm(alpha * x_ref[...] + acc_ref[...], g_ref[...], b_ref[...])


def _mlp(x2, w1, w2, layer, ln_g, ln_b, *, alpha):
    n, d = x2.shape
    dff = w1.shape[2]
    tm = min(512, n)
    tf = min(1024, dff)
    return pl.pallas_call(
        functools.partial(_mlp_kernel, alpha=alpha),
        grid=(n // tm, dff // tf),
        in_specs=[
            pl.BlockSpec((tm, d), lambda i, f: (i, 0)),
            pl.BlockSpec((None, d, tf), lambda i, f: (layer, 0, f)),
            pl.BlockSpec((None, tf, d), lambda i, f: (layer, f, 0)),
            pl.BlockSpec((1, d), lambda i, f: (0, 0)),
            pl.BlockSpec((1, d), lambda i, f: (0, 0)),
        ],
        out_specs=pl.BlockSpec((tm, d), lambda i, f: (i, 0)),
        out_shape=jax.ShapeDtypeStruct((n, d), F32),
        scratch_shapes=[pltpu.VMEM((tm, d), BF16), pltpu.VMEM((tm, d), F32)],
        compiler_params=_params("parallel", "arbitrary"),
        name="mlp",
    )(x2, w1, w2, ln_g.reshape(1, d), ln_b.reshape(1, d))


def _rope_tables(positions):
    inv = ROPE_THETA ** (-jnp.arange(0, QK_DIM, 2, dtype=F32) / QK_DIM)
    ang = positions.astype(F32).reshape(-1, 1) * inv
    cos = jnp.cos(ang)
    sin = jnp.sin(ang)
    reps = LANES // QK_DIM
    return (jnp.tile(jnp.concatenate([cos, cos], axis=1), (1, reps)),
            jnp.tile(jnp.concatenate([-sin, sin], axis=1), (1, reps)))


def kernel(x, positions, w_in, lambda_qk, subln_g, ssm_a_re, ssm_a_im, ssm_log_dt, ssm_b_re,
           ssm_b_im, ssm_c_re, ssm_c_im, ssm_d, w_glu, w_attn_up, w_ssm_up, w_out, ln1_g, ln1_b,
           ln2_g, ln2_b, w_mlp_up, w_mlp_down):
    bsz, seq, d = x.shape
    depth = w_in.shape[0]
    attn_w = w_attn_up.shape[1]
    ssm_w = w_ssm_up.shape[1]
    heads = attn_w // V_DIM
    q_cols = heads * HEAD_COLS
    assert w_in.shape[2] == 2 * q_cols + attn_w + ssm_w + 2 * d
    assert q_cols == attn_w == ssm_w and d % ssm_w == 0
    alpha = (2.0 * depth) ** 0.25
    chunk = min(512, seq)

    cos_t, sin_t = _rope_tables(positions)
    w_in_bf, w_glu_bf, w_attn_up_bf, w_ssm_up_bf, w_out_bf, w_mlp_up_bf, w_mlp_down_bf = (
        w.astype(BF16) for w in (w_in, w_glu, w_attn_up, w_ssm_up, w_out, w_mlp_up, w_mlp_down))
    x2 = x.reshape(bsz * seq, d)
    for l in range(depth):
        lam_init = _lambda_init(l)
        proj = _in_proj(x2, w_in_bf, l, cos_t, sin_t, q_cols=q_cols,
                        rope_cols=2 * q_cols, plain_cols=attn_w + ssm_w)
        attn = _diff_attn(proj, lambda_qk[l], subln_g[l], bsz=bsz, seq=seq, heads=heads,
                          lam_init=lam_init)
        prep = _ssm_prep(ssm_a_re[l], ssm_a_im[l], ssm_log_dt[l], ssm_b_re[l], ssm_b_im[l],
                         sub=chunk // SUBLANES)
        ssm = _s5_ssm(proj, (2 * q_cols + attn_w) // ssm_w, prep, ssm_c_re[l], ssm_c_im[l],
                      ssm_d[l], w_glu_bf, l, bsz=bsz, seq=seq, chunk=chunk)
        x2 = _mix_out(x2, attn, ssm, proj, (2 * q_cols + attn_w + ssm_w) // d,
                      w_attn_up_bf, w_ssm_up_bf, w_out_bf, l, ln1_g[l], ln1_b[l], alpha=alpha)
        x2 = _mlp(x2, w_mlp_up_bf, w_mlp_down_bf, l, ln2_g[l], ln2_b[l], alpha=alpha)
    return x2.reshape(bsz, seq, d)
```

```python
import functools
import math

import jax
import jax.numpy as jnp
from jax import lax
from jax.experimental import pallas as pl
from jax.experimental.pallas import tpu as pltpu

F32 = jnp.float32
BF16 = jnp.bfloat16

QK_DIM = 64
V_DIM = 2 * QK_DIM
HEAD_COLS = 2 * QK_DIM
SSM_GROUP = 16
SSM_STATE = 64
ROPE_THETA = 10000.0
LN_EPS = 1e-5
RMS_EPS = 1e-5
LANES = 128
SUBLANES = 8
VMEM_LIMIT = 56 * 1024 * 1024
NEG_BIG = -1e30


def _lambda_init(layer):
    return 0.8 - 0.6 * math.exp(-0.3 * layer)


def _params(*sem):
    return pltpu.CompilerParams(dimension_semantics=sem, vmem_limit_bytes=VMEM_LIMIT)


EPILOGUE_COLS = 256


def _in_proj_kernel(x_ref, w_ref, cos_ref, sin_ref, o_ref, xb_ref, *, q_tiles, rope_tiles,
                    plain_tiles):
    j = pl.program_id(1)
    tn = o_ref.shape[1]
    chunks = [slice(c0, c0 + EPILOGUE_COLS) for c0 in range(0, tn, EPILOGUE_COLS)]

    @pl.when(j == 0)
    def _():
        xb_ref[...] = x_ref[...].astype(BF16)

    def matmul(cs):
        return jnp.dot(xb_ref[...], w_ref[:, cs], preferred_element_type=F32)

    @pl.when(j < rope_tiles)
    def _():
        scale = jnp.where(j < q_tiles, QK_DIM ** -0.5 * math.log2(math.e), 1.0).astype(F32)
        reps = EPILOGUE_COLS // LANES
        c = jnp.tile(cos_ref[...] * scale, (1, reps))
        s = jnp.tile(sin_ref[...] * scale, (1, reps))
        lane = lax.broadcasted_iota(jnp.int32, c.shape, 1)
        first_half = (lane % QK_DIM) < (QK_DIM // 2)
        for cs in chunks:
            acc = matmul(cs)
            rot = jnp.where(first_half, pltpu.roll(acc, EPILOGUE_COLS - QK_DIM // 2, 1),
                            pltpu.roll(acc, QK_DIM // 2, 1))
            o_ref[:, cs] = (acc * c + rot * s).astype(o_ref.dtype)

    @pl.when((j >= rope_tiles) & (j < rope_tiles + plain_tiles))
    def _():
        for cs in chunks:
            o_ref[:, cs] = matmul(cs).astype(o_ref.dtype)

    @pl.when(j >= rope_tiles + plain_tiles)
    def _():
        for cs in chunks:
            o_ref[:, cs] = jax.nn.sigmoid(matmul(cs)).astype(o_ref.dtype)


def _in_proj(x2, w_bf, layer, cos_t, sin_t, *, q_cols, rope_cols, plain_cols):
    n, d = x2.shape
    cols = w_bf.shape[2]
    tm = min(1024, n)
    tn = min(1024, q_cols)
    kern = functools.partial(_in_proj_kernel, q_tiles=q_cols // tn, rope_tiles=rope_cols // tn,
                             plain_tiles=plain_cols // tn)
    return pl.pallas_call(
        kern,
        grid=(n // tm, cols // tn),
        in_specs=[
            pl.BlockSpec((tm, d), lambda i, j: (i, 0)),
            pl.BlockSpec((None, d, tn), lambda i, j: (layer, 0, j)),
            pl.BlockSpec((tm, LANES), lambda i, j: (i, 0)),
            pl.BlockSpec((tm, LANES), lambda i, j: (i, 0)),
        ],
        out_specs=pl.BlockSpec((tm, tn), lambda i, j: (i, j)),
        out_shape=jax.ShapeDtypeStruct((n, cols), BF16),
        scratch_shapes=[pltpu.VMEM((tm, d), BF16)],
        compiler_params=_params("parallel", "arbitrary"),
        name="in_proj",
    )(x2, w_bf, cos_t, sin_t)


ONES_ROWS = 16


def _attn_kernel(lq_ref, g_ref, q_ref, k_ref, v_ref, o_ref, vt_ref, qt_ref, sa_ref, sb_ref, bma_ref, bmb_ref,
                 m_ref, acc_ref, *, lam_init):
    g = pl.program_id(2)
    t = q_ref.shape[0] // 2
    seq = k_ref.shape[0]
    cols_all, cols_a, cols_b = slice(0, 2 * t), slice(0, t), slice(t, 2 * t)

    @pl.when(g == 0)
    def _():
        for r in range(seq // t):
            rows = slice(r * t, (r + 1) * t)
            vt_ref[0:V_DIM, rows] = v_ref[rows, :].astype(F32).T.astype(BF16)
        vt_ref[V_DIM:V_DIM + ONES_ROWS, :] = jnp.ones((ONES_ROWS, seq), BF16)

    qt = q_ref[...].astype(F32).T
    comp = lax.broadcasted_iota(jnp.int32, qt.shape, 0) // QK_DIM
    for c in range(2):
        qt_ref[c] = jnp.where(comp == c, qt, 0.0).astype(BF16)
    m_ref[...] = jnp.full(m_ref.shape, NEG_BIG, F32)
    acc_ref[...] = jnp.zeros(acc_ref.shape, F32)

    def scores(ki, buf, cols, masked):
        s_ref, bm_ref = buf
        k = k_ref[pl.ds(pl.multiple_of(ki * t, t), t), :]
        if masked:
            kpos = lax.broadcasted_iota(jnp.int32, (t, t), 0)
            qpos = lax.broadcasted_iota(jnp.int32, (t, t), 1)
            keep = kpos <= qpos
        for c in range(2):
            st = jnp.dot(k, qt_ref[c, :, cols], preferred_element_type=F32)
            if masked:
                st = jnp.where(keep, st, NEG_BIG)
            s_ref[c, :, cols] = st
            bm_ref[c:c + 1, cols] = jnp.max(st, axis=0, keepdims=True)

    def softmax_pv(ki, buf, cols):
        s_ref, bm_ref = buf
        vt = vt_ref[:, pl.ds(pl.multiple_of(ki * t, t), t)]
        for c in range(2):
            m_prev = m_ref[c:c + 1, cols]
            m_new = jnp.maximum(m_prev, bm_ref[c:c + 1, cols])
            alpha = jnp.exp2(m_prev - m_new)
            p = jnp.exp2(s_ref[c, :, cols] - m_new).astype(BF16)
            acc_ref[c, :, cols] = (alpha * acc_ref[c, :, cols]
                                   + jnp.dot(vt, p, preferred_element_type=F32))
            m_ref[c:c + 1, cols] = m_new

    buf_a = (sa_ref, bma_ref)
    buf_b = (sb_ref, bmb_ref)

    def diag_a_block(buf):
        scores(2 * g, buf, cols_a, True)
        scores(2 * g, buf, cols_b, False)

    @pl.when(g == 0)
    def _():
        diag_a_block(buf_a)

    @pl.when(g > 0)
    def _():
        scores(0, buf_a, cols_all, False)

    def pair(i, carry):
        scores(2 * i + 1, buf_b, cols_all, False)
        softmax_pv(2 * i, buf_a, cols_all)
        scores(2 * i + 2, buf_a, cols_all, False)
        softmax_pv(2 * i + 1, buf_b, cols_all)
        return carry

    lax.fori_loop(0, jnp.maximum(g - 1, 0), pair, 0)

    @pl.when(g > 0)
    def _():
        scores(2 * g - 1, buf_b, cols_all, False)
        softmax_pv(2 * g - 2, buf_a, cols_all)
        diag_a_block(buf_a)
        softmax_pv(2 * g - 1, buf_b, cols_all)

    scores(2 * g + 1, buf_b, cols_b, True)
    softmax_pv(2 * g, buf_a, cols_all)
    softmax_pv(2 * g + 1, buf_b, cols_b)

    lq = lq_ref[...]
    lam = (jnp.exp(jnp.sum(lq[0:1] * lq[1:2], axis=1, keepdims=True))
           - jnp.exp(jnp.sum(lq[2:3] * lq[3:4], axis=1, keepdims=True)) + lam_init)
    o1 = acc_ref[0, 0:V_DIM, :] / acc_ref[0, V_DIM:V_DIM + 1, :]
    o2 = acc_ref[1, 0:V_DIM, :] / acc_ref[1, V_DIM:V_DIM + 1, :]
    out_t = o1 - lam * o2
    out_t = out_t * lax.rsqrt(jnp.mean(out_t * out_t, axis=0, keepdims=True) + RMS_EPS)
    out = out_t.T * g_ref[...] * (1.0 - lam_init)
    o_ref[...] = out.astype(o_ref.dtype)


def _diff_attn(proj, lam_qk, subln_g, *, bsz, seq, heads, lam_init):
    n = proj.shape[0]
    t = min(512, seq // 2)
    tq = 2 * t
    nq = seq // tq
    kern = functools.partial(_attn_kernel, lam_init=lam_init)
    return pl.pallas_call(
        kern,
        grid=(bsz, heads, nq),
        in_specs=[
            pl.BlockSpec((4, QK_DIM), lambda b, h, qi: (0, 0)),
            pl.BlockSpec((1, V_DIM), lambda b, h, qi: (0, 0)),
            pl.BlockSpec((tq, HEAD_COLS), lambda b, h, qi: (b * nq + qi, h)),
            pl.BlockSpec((seq, HEAD_COLS), lambda b, h, qi: (b, heads + h)),
            pl.BlockSpec((seq, V_DIM), lambda b, h, qi: (b, 2 * heads + h)),
        ],
        out_specs=pl.BlockSpec((tq, V_DIM), lambda b, h, qi: (b * nq + qi, h)),
        out_shape=jax.ShapeDtypeStruct((n, heads * V_DIM), BF16),
        scratch_shapes=[
            pltpu.VMEM((V_DIM + ONES_ROWS, seq), BF16),
            pltpu.VMEM((2, HEAD_COLS, tq), BF16),
            pltpu.VMEM((2, t, tq), F32), pltpu.VMEM((2, t, tq), F32),
            pltpu.VMEM((2, tq), F32), pltpu.VMEM((2, tq), F32),
            pltpu.VMEM((2, tq), F32), pltpu.VMEM((2, V_DIM + ONES_ROWS, tq), F32),
        ],
        compiler_params=_params("parallel", "parallel", "arbitrary"),
        name="diff_attn",
    )(lam_qk, subln_g.reshape(1, V_DIM), proj, proj, proj)


def _cmul(ar, ai, br, bi):
    return ar * br - ai * bi, ar * bi + ai * br


def _ssm_prep_kernel(are_ref, aim_ref, ldt_ref, bre_ref, bim_ref,
                     bbre_ref, bbim_ref, hre_ref, him_ref, *, log2_sub):
    a_re = are_ref[...]
    a_im = aim_ref[...]
    dt = jnp.exp(ldt_ref[...])
    mag = jnp.exp(dt * a_re)
    ab_re = mag * jnp.cos(dt * a_im)
    ab_im = mag * jnp.sin(dt * a_im)
    den = a_re * a_re + a_im * a_im
    nr, ni = ab_re - 1.0, ab_im
    z_re = (nr * a_re + ni * a_im) / den
    z_im = (ni * a_re - nr * a_im) / den
    b_re = bre_ref[...]
    b_im = bim_ref[...]
    bbre_ref[...] = z_re * b_re - z_im * b_im
    bbim_ref[...] = z_re * b_im + z_im * b_re
    hre_ref[0:1, :] = ab_re
    him_ref[0:1, :] = ab_im
    er, ei = ab_re, ab_im
    for _ in range(log2_sub):
        er, ei = _cmul(er, ei, er, ei)
    for r in range(1, 4):
        hre_ref[r:r + 1, :] = er
        him_ref[r:r + 1, :] = ei
        er, ei = _cmul(er, ei, er, ei)


def _ssm_prep(a_re, a_im, log_dt, b_re, b_im, *, sub):
    g, p = a_re.shape
    w = g * p
    log2_sub = sub.bit_length() - 1
    assert sub == 1 << log2_sub
    row = lambda t: t.reshape(1, w)
    ldt = jnp.broadcast_to(log_dt[:, None], (g, p))
    tb = lambda t: jnp.transpose(t, (2, 0, 1)).reshape(SSM_GROUP, w)
    full = lambda r: pl.BlockSpec((r, w), lambda: (0, 0))
    shapes = [(SSM_GROUP, w), (SSM_GROUP, w), (4, w), (4, w)]
    return pl.pallas_call(
        functools.partial(_ssm_prep_kernel, log2_sub=log2_sub),
        in_specs=[full(1), full(1), full(1), full(SSM_GROUP), full(SSM_GROUP)],
        out_specs=[full(s[0]) for s in shapes],
        out_shape=[jax.ShapeDtypeStruct(s, F32) for s in shapes],
        compiler_params=pltpu.CompilerParams(vmem_limit_bytes=VMEM_LIMIT),
        name="ssm_prep",
    )(row(a_re), row(a_im), row(ldt), tb(b_re), tb(b_im))


def _block_diag_tiles(blocks, per_tile):
    g, r, c = blocks.shape
    t = blocks.reshape(g // per_tile, per_tile, r, c)
    eye = jnp.eye(per_tile, dtype=blocks.dtype)
    out = t[:, :, :, None, :] * eye[None, :, None, :, None]
    return out.reshape(g // per_tile, per_tile * r, per_tile * c)


GROUPS_PER_TILE = LANES // SSM_GROUP
STATES_PER_TILE = GROUPS_PER_TILE * SSM_STATE


def _ssm_kernel(u_ref, perm_ref, permt_ref, bb_ref, cre_ref, cim_ref, d_ref, wglu_ref,
                hre_ref, him_ref, o_ref, xr_ref, xi_ref, xb_ref, hr_ref, hi_ref):
    ci = pl.program_id(1)
    chunk = u_ref.shape[0]
    sub = chunk // SUBLANES
    width = xr_ref.shape[1]
    n_tiles = width // STATES_PER_TILE
    bcast = lambda row: jnp.broadcast_to(row, (SUBLANES, STATES_PER_TILE))

    @pl.when(ci == 0)
    def _():
        hr_ref[...] = jnp.zeros(hr_ref.shape, F32)
        hi_ref[...] = jnp.zeros(hi_ref.shape, F32)

    u = jnp.dot(perm_ref[...], u_ref[...], preferred_element_type=F32)
    ub = u.astype(BF16)
    row8 = lax.broadcasted_iota(jnp.int32, (SUBLANES, STATES_PER_TILE), 0)

    def drive(t):
        bu = jnp.dot(ub[:, t * LANES:(t + 1) * LANES], bb_ref[t], preferred_element_type=F32)
        sl = slice(t * STATES_PER_TILE, (t + 1) * STATES_PER_TILE)
        xr_ref[:, sl] = bu[:, :STATES_PER_TILE]
        xi_ref[:, sl] = bu[:, STATES_PER_TILE:]

    def scan(t):
        sl = slice(t * STATES_PER_TILE, (t + 1) * STATES_PER_TILE)
        sli = slice(width + sl.start, width + sl.stop)
        ar = bcast(hre_ref[0:1, sl])
        ai = bcast(him_ref[0:1, sl])
        xr = jnp.where(row8 == 0, bcast(hr_ref[:, sl]), 0.0)
        xi = jnp.where(row8 == 0, bcast(hi_ref[:, sl]), 0.0)
        for j in range(sub):
            rows = slice(j * SUBLANES, (j + 1) * SUBLANES)
            xr, xi = (ar * xr - ai * xi + xr_ref[rows, sl], ar * xi + ai * xr + xi_ref[rows, sl])
            xr_ref[rows, sl] = xr
            xi_ref[rows, sl] = xi
        er, ei = xr, xi
        for r, d in enumerate((1, 2, 4), start=1):
            pr = bcast(hre_ref[r:r + 1, sl])
            pi = bcast(him_ref[r:r + 1, sl])
            sr = jnp.where(row8 >= d, pltpu.roll(er, d, 0), 0.0)
            si = jnp.where(row8 >= d, pltpu.roll(ei, d, 0), 0.0)
            er, ei = er + pr * sr - pi * si, ei + pr * si + pi * sr
        hr_ref[:, sl] = er[SUBLANES - 1:SUBLANES, :]
        hi_ref[:, sl] = ei[SUBLANES - 1:SUBLANES, :]
        cr = jnp.where(row8 >= 1, pltpu.roll(er, 1, 0), 0.0)
        cim = jnp.where(row8 >= 1, pltpu.roll(ei, 1, 0), 0.0)
        for j in range(0, sub, 2):
            halves_r, halves_i = [], []
            for jj in (j, j + 1):
                rows = slice(jj * SUBLANES, (jj + 1) * SUBLANES)
                cr, cim = ar * cr - ai * cim, ar * cim + ai * cr
                halves_r.append(xr_ref[rows, sl] + cr)
                halves_i.append(xi_ref[rows, sl] + cim)
            rows2 = slice(j * SUBLANES, (j + 2) * SUBLANES)
            xb_ref[rows2, sl] = jnp.concatenate(halves_r, axis=0).astype(BF16)
            xb_ref[rows2, sli] = jnp.concatenate(halves_i, axis=0).astype(BF16)

    def readout(t):
        sl = slice(t * STATES_PER_TILE, (t + 1) * STATES_PER_TILE)
        sli = slice(width + sl.start, width + sl.stop)
        return (jnp.dot(xb_ref[:, sl], cre_ref[t], preferred_element_type=F32)
                - jnp.dot(xb_ref[:, sli], cim_ref[t], preferred_element_type=F32))

    ys = []
    drive(0)
    for t in range(n_tiles):
        if t + 1 < n_tiles:
            drive(t + 1)
        scan(t)
        ys.append(readout(t))
    y = jnp.concatenate(ys, axis=1) + d_ref[...] * u
    g = jax.nn.gelu(y)
    z = jnp.dot(g.astype(BF16), wglu_ref[...], preferred_element_type=F32)
    out = (g * jax.nn.sigmoid(z)).astype(BF16)
    o_ref[...] = jnp.dot(permt_ref[...], out, preferred_element_type=F32).astype(o_ref.dtype)


def _s5_ssm(proj, u_col_block, prep, c_re, c_im, d_skip, w_glu_bf, layer, *, bsz, seq, chunk):
    bb_re, bb_im, hs_re, hs_im = prep
    n = proj.shape[0]
    groups = c_re.shape[0]
    ssm_w = groups * SSM_GROUP
    width = groups * SSM_STATE
    n_tiles = ssm_w // LANES
    sub = chunk // SUBLANES
    nc = seq // chunk
    to_blocks = lambda t: jnp.transpose(t.reshape(SSM_GROUP, groups, SSM_STATE), (1, 0, 2))
    bb = jnp.concatenate([_block_diag_tiles(to_blocks(bb_re), GROUPS_PER_TILE),
                          _block_diag_tiles(to_blocks(bb_im), GROUPS_PER_TILE)], axis=2).astype(BF16)
    ct = lambda t: _block_diag_tiles(jnp.transpose(t, (0, 2, 1)), GROUPS_PER_TILE).astype(BF16)
    r = jnp.arange(chunk)
    src = (r % SUBLANES) * sub + r // SUBLANES
    perm = (src[:, None] == jnp.arange(chunk)[None, :]).astype(BF16)
    const2 = lambda shape: pl.BlockSpec(shape, lambda b, c: (0, 0))
    const3 = lambda shape: pl.BlockSpec(shape, lambda b, c: (0, 0, 0))
    return pl.pallas_call(
        _ssm_kernel,
        grid=(bsz, nc),
        in_specs=[
            pl.BlockSpec((chunk, ssm_w), lambda b, c: (b * nc + c, u_col_block)),
            const2((chunk, chunk)), const2((chunk, chunk)),
            const3((n_tiles, LANES, 2 * STATES_PER_TILE)),
            const3((n_tiles, STATES_PER_TILE, LANES)), const3((n_tiles, STATES_PER_TILE, LANES)),
            const2((1, ssm_w)), pl.BlockSpec((None, ssm_w, ssm_w), lambda b, c: (layer, 0, 0)),
            const2((4, width)), const2((4, width)),
        ],
        out_specs=pl.BlockSpec((chunk, ssm_w), lambda b, c: (b * nc + c, 0)),
        out_shape=jax.ShapeDtypeStruct((n, ssm_w), BF16),
        scratch_shapes=[
            pltpu.VMEM((chunk, width), F32), pltpu.VMEM((chunk, width), F32),
            pltpu.VMEM((chunk, 2 * width), BF16),
            pltpu.VMEM((1, width), F32), pltpu.VMEM((1, width), F32),
        ],
        compiler_params=_params("parallel", "arbitrary"),
        name="s5_ssm",
    )(proj, perm, perm.T, bb, ct(c_re), ct(c_im), d_skip.reshape(1, ssm_w), w_glu_bf, hs_re, hs_im)


def _layer_norm(y, g, b):
    mu = jnp.mean(y, axis=1, keepdims=True)
    yc = y - mu
    var = jnp.mean(yc * yc, axis=1, keepdims=True)
    return yc * lax.rsqrt(var + LN_EPS) * g + b


def _mix_kernel(x_ref, attn_ref, ssm_ref, ga_ref, gs_ref, wa_ref, ws_ref, wo_ref, g_ref, b_ref,
                o_ref, *, alpha):
    a = jnp.dot(attn_ref[...], wa_ref[...], preferred_element_type=F32)
    s = jnp.dot(ssm_ref[...], ws_ref[...], preferred_element_type=F32)
    merged = ga_ref[...].astype(F32) * a + gs_ref[...].astype(F32) * s
    mix = jnp.dot(merged.astype(BF16), wo_ref[...], preferred_element_type=F32)
    o_ref[...] = _layer_norm(alpha * x_ref[...] + mix, g_ref[...], b_ref[...])


def _mix_out(x2, attn, ssm, proj, ga_block, wa, ws, wo, layer, ln_g, ln_b, *, alpha):
    n, d = x2.shape
    aw = attn.shape[1]
    sw = ssm.shape[1]
    tm = min(256, n)
    row = lambda w: pl.BlockSpec((tm, w), lambda i: (i, 0))
    const = lambda shape: pl.BlockSpec(shape, lambda i: (0, 0), pipeline_mode=pl.Buffered(1))
    wconst = lambda shape: pl.BlockSpec((None,) + shape, lambda i: (layer, 0, 0),
                                        pipeline_mode=pl.Buffered(1))
    return pl.pallas_call(
        functools.partial(_mix_kernel, alpha=alpha),
        grid=(n // tm,),
        in_specs=[
            row(d), row(aw), row(sw),
            pl.BlockSpec((tm, d), lambda i: (i, ga_block)),
            pl.BlockSpec((tm, d), lambda i: (i, ga_block + 1)),
            wconst((aw, d)), wconst((sw, d)), wconst((d, d)), const((1, d)), const((1, d)),
        ],
        out_specs=row(d),
        out_shape=jax.ShapeDtypeStruct((n, d), F32),
        compiler_params=_params("parallel"),
        name="mix_out",
    )(x2, attn, ssm, proj, proj, wa, ws, wo, ln_g.reshape(1, d), ln_b.reshape(1, d))


def _mlp_kernel(x_ref, w1_ref, w2_ref, g_ref, b_ref, o_ref, xb_ref, acc_ref, *, alpha):
    f = pl.program_id(1)

    @pl.when(f == 0)
    def _():
        xb_ref[...] = x_ref[...].astype(BF16)
        acc_ref[...] = jnp.zeros(acc_ref.shape, F32)

    h = jnp.dot(xb_ref[...], w1_ref[...], preferred_element_type=F32)
    h = jnp.square(jnp.maximum(h, 0.0))
    acc_ref[...] += jnp.dot(h.astype(BF16), w2_ref[...], preferred_element_type=F32)

    @pl.when(f == pl.num_programs(1) - 1)
    def _():
        o_ref[...] = _layer_norm(alpha * x_ref[...] + acc_ref[...], g_ref[...], b_ref[...])


def _mlp(x2, w1, w2, layer, ln_g, ln_b, *, alpha):
    n, d = x2.shape
    dff = w1.shape[2]
    tm = min(512, n)
    tf = min(1024, dff)
    return pl.pallas_call(
        functools.partial(_mlp_kernel, alpha=alpha),
        grid=(n // tm, dff // tf),
        in_specs=[
            pl.BlockSpec((tm, d), lambda i, f: (i, 0)),
            pl.BlockSpec((None, d, tf), lambda i, f: (layer, 0, f)),
            pl.BlockSpec((None, tf, d), lambda i, f: (layer, f, 0)),
            pl.BlockSpec((1, d), lambda i, f: (0, 0)),
            pl.BlockSpec((1, d), lambda i, f: (0, 0)),
        ],
        out_specs=pl.BlockSpec((tm, d), lambda i, f: (i, 0)),
        out_shape=jax.ShapeDtypeStruct((n, d), F32),
        scratch_shapes=[pltpu.VMEM((tm, d), BF16), pltpu.VMEM((tm, d), F32)],
        compiler_params=_params("parallel", "arbitrary"),
        name="mlp",
    )(x2, w1, w2, ln_g.reshape(1, d), ln_b.reshape(1, d))


def _rope_tables(positions):
    inv = ROPE_THETA ** (-jnp.arange(0, QK_DIM, 2, dtype=F32) / QK_DIM)
    ang = positions.astype(F32).reshape(-1, 1) * inv
    cos = jnp.cos(ang)
    sin = jnp.sin(ang)
    reps = LANES // QK_DIM
    return (jnp.tile(jnp.concatenate([cos, cos], axis=1), (1, reps)),
            jnp.tile(jnp.concatenate([-sin, sin], axis=1), (1, reps)))


def kernel(x, positions, w_in, lambda_qk, subln_g, ssm_a_re, ssm_a_im, ssm_log_dt, ssm_b_re,
           ssm_b_im, ssm_c_re, ssm_c_im, ssm_d, w_glu, w_attn_up, w_ssm_up, w_out, ln1_g, ln1_b,
           ln2_g, ln2_b, w_mlp_up, w_mlp_down):
    bsz, seq, d = x.shape
    depth = w_in.shape[0]
    attn_w = w_attn_up.shape[1]
    ssm_w = w_ssm_up.shape[1]
    heads = attn_w // V_DIM
    q_cols = heads * HEAD_COLS
    assert w_in.shape[2] == 2 * q_cols + attn_w + ssm_w + 2 * d
    assert q_cols == attn_w == ssm_w and d % ssm_w == 0
    alpha = (2.0 * depth) ** 0.25
    chunk = min(512, seq)

    cos_t, sin_t = _rope_tables(positions)
    w_in_bf, w_glu_bf, w_attn_up_bf, w_ssm_up_bf, w_out_bf, w_mlp_up_bf, w_mlp_down_bf = (
        w.astype(BF16) for w in (w_in, w_glu, w_attn_up, w_ssm_up, w_out, w_mlp_up, w_mlp_down))
    x2 = x.reshape(bsz * seq, d)
    for l in range(depth):
        lam_init = _lambda_init(l)
        proj = _in_proj(x2, w_in_bf, l, cos_t, sin_t, q_cols=q_cols,
                        rope_cols=2 * q_cols, plain_cols=attn_w + ssm_w)
        attn = _diff_attn(proj, lambda_qk[l], subln_g[l], bsz=bsz, seq=seq, heads=heads,
                          lam_init=lam_init)
        prep = _ssm_prep(ssm_a_re[l], ssm_a_im[l], ssm_log_dt[l], ssm_b_re[l], ssm_b_im[l],
                         sub=chunk // SUBLANES)
        ssm = _s5_ssm(proj, (2 * q_cols + attn_w) // ssm_w, prep, ssm_c_re[l], ssm_c_im[l],
                      ssm_d[l], w_glu_bf, l, bsz=bsz, seq=seq, chunk=chunk)
        x2 = _mix_out(x2, attn, ssm, proj, (2 * q_cols + attn_w + ssm_w) // d,
                      w_attn_up_bf, w_ssm_up_bf, w_out_bf, l, ln1_g[l], ln1_b[l], alpha=alpha)
        x2 = _mlp(x2, w_mlp_up_bf, w_mlp_down_bf, l, ln2_g[l], ln2_b[l], alpha=alpha)
    return x2.reshape(bsz, seq, d)
```

```python
import functools
import math

import jax
import jax.numpy as jnp
from jax import lax
from jax.experimental import pallas as pl
from jax.experimental.pallas import tpu as pltpu

F32 = jnp.float32
BF16 = jnp.bfloat16

QK_DIM = 64
V_DIM = 2 * QK_DIM
HEAD_COLS = 2 * QK_DIM
SSM_GROUP = 16
SSM_STATE = 64
ROPE_THETA = 10000.0
LN_EPS = 1e-5
RMS_EPS = 1e-5
LANES = 128
SUBLANES = 8
VMEM_LIMIT = 56 * 1024 * 1024
NEG_BIG = -1e30


def _lambda_init(layer):
    return 0.8 - 0.6 * math.exp(-0.3 * layer)


def _params(*sem):
    return pltpu.CompilerParams(dimension_semantics=sem, vmem_limit_bytes=VMEM_LIMIT)


EPILOGUE_COLS = 256


def _in_proj_kernel(x_ref, w_ref, cos_ref, sin_ref, o_ref, xb_ref, *, q_tiles, rope_tiles,
                    plain_tiles):
    j = pl.program_id(1)
    tn = o_ref.shape[1]
    chunks = [slice(c0, c0 + EPILOGUE_COLS) for c0 in range(0, tn, EPILOGUE_COLS)]

    @pl.when(j == 0)
    def _():
        xb_ref[...] = x_ref[...].astype(BF16)

    def matmul(cs):
        return jnp.dot(xb_ref[...], w_ref[:, cs], preferred_element_type=F32)

    @pl.when(j < rope_tiles)
    def _():
        scale = jnp.where(j < q_tiles, QK_DIM ** -0.5 * math.log2(math.e), 1.0).astype(F32)
        reps = EPILOGUE_COLS // LANES
        c = jnp.tile(cos_ref[...] * scale, (1, reps))
        s = jnp.tile(sin_ref[...] * scale, (1, reps))
        lane = lax.broadcasted_iota(jnp.int32, c.shape, 1)
        first_half = (lane % QK_DIM) < (QK_DIM // 2)
        for cs in chunks:
            acc = matmul(cs)
            rot = jnp.where(first_half, pltpu.roll(acc, EPILOGUE_COLS - QK_DIM // 2, 1),
                            pltpu.roll(acc, QK_DIM // 2, 1))
            o_ref[:, cs] = (acc * c + rot * s).astype(o_ref.dtype)

    @pl.when((j >= rope_tiles) & (j < rope_tiles + plain_tiles))
    def _():
        for cs in chunks:
            o_ref[:, cs] = matmul(cs).astype(o_ref.dtype)

    @pl.when(j >= rope_tiles + plain_tiles)
    def _():
        for cs in chunks:
            o_ref[:, cs] = jax.nn.sigmoid(matmul(cs)).astype(o_ref.dtype)


def _in_proj(x2, w_bf, layer, cos_t, sin_t, *, q_cols, rope_cols, plain_cols):
    n, d = x2.shape
    cols = w_bf.shape[2]
    tm = min(1024, n)
    tn = min(1024, q_cols)
    kern = functools.partial(_in_proj_kernel, q_tiles=q_cols // tn, rope_tiles=rope_cols // tn,
                             plain_tiles=plain_cols // tn)
    return pl.pallas_call(
        kern,
        grid=(n // tm, cols // tn),
        in_specs=[
            pl.BlockSpec((tm, d), lambda i, j: (i, 0)),
            pl.BlockSpec((None, d, tn), lambda i, j: (layer, 0, j)),
            pl.BlockSpec((tm, LANES), lambda i, j: (i, 0)),
            pl.BlockSpec((tm, LANES), lambda i, j: (i, 0)),
        ],
        out_specs=pl.BlockSpec((tm, tn), lambda i, j: (i, j)),
        out_shape=jax.ShapeDtypeStruct((n, cols), BF16),
        scratch_shapes=[pltpu.VMEM((tm, d), BF16)],
        compiler_params=_params("parallel", "arbitrary"),
        name="in_proj",
    )(x2, w_bf, cos_t, sin_t)


ONES_ROWS = 16


def _attn_kernel(lq_ref, g_ref, q_ref, k_ref, v_ref, o_ref, vt_ref, qt_ref, sa_ref, sb_ref, bma_ref, bmb_ref,
                 m_ref, acc_ref, *, lam_init):
    g = pl.program_id(2)
    t = q_ref.shape[0] // 2
    seq = k_ref.shape[0]
    cols_all, cols_a, cols_b = slice(0, 2 * t), slice(0, t), slice(t, 2 * t)

    @pl.when(g == 0)
    def _():
        for r in range(seq // t):
            rows = slice(r * t, (r + 1) * t)
            vt_ref[0:V_DIM, rows] = v_ref[rows, :].astype(F32).T.astype(BF16)
        vt_ref[V_DIM:V_DIM + ONES_ROWS, :] = jnp.ones((ONES_ROWS, seq), BF16)

    qt = q_ref[...].astype(F32).T
    comp = lax.broadcasted_iota(jnp.int32, qt.shape, 0) // QK_DIM
    for c in range(2):
        qt_ref[c] = jnp.where(comp == c, qt, 0.0).astype(BF16)
    m_ref[...] = jnp.full(m_ref.shape, NEG_BIG, F32)
    acc_ref[...] = jnp.zeros(acc_ref.shape, F32)

    def scores(ki, buf, cols, masked):
        s_ref, bm_ref = buf
        k = k_ref[pl.ds(pl.multiple_of(ki * t, t), t), :]
        if masked:
            kpos = lax.broadcasted_iota(jnp.int32, (t, t), 0)
            qpos = lax.broadcasted_iota(jnp.int32, (t, t), 1)
            keep = kpos <= qpos
        for c in range(2):
            st = jnp.dot(k, qt_ref[c, :, cols], preferred_element_type=F32)
            if masked:
                st = jnp.where(keep, st, NEG_BIG)
            s_ref[c, :, cols] = st
            bm_ref[c:c + 1, cols] = jnp.max(st, axis=0, keepdims=True)

    def softmax_pv(ki, buf, cols):
        s_ref, bm_ref = buf
        vt = vt_ref[:, pl.ds(pl.multiple_of(ki * t, t), t)]
        for c in range(2):
            m_prev = m_ref[c:c + 1, cols]
            m_new = jnp.maximum(m_prev, bm_ref[c:c + 1, cols])
            alpha = jnp.exp2(m_prev - m_new)
            p = jnp.exp2(s_ref[c, :, cols] - m_new).astype(BF16)
            acc_ref[c, :, cols] = (alpha * acc_ref[c, :, cols]
                                   + jnp.dot(vt, p, preferred_element_type=F32))
            m_ref[c:c + 1, cols] = m_new

    buf_a = (sa_ref, bma_ref)
    buf_b = (sb_ref, bmb_ref)

    def diag_a_block(buf):
        scores(2 * g, buf, cols_a, True)
        scores(2 * g, buf, cols_b, False)

    @pl.when(g == 0)
    def _():
        diag_a_block(buf_a)

    @pl.when(g > 0)
    def _():
        scores(0, buf_a, cols_all, False)

    def pair(i, carry):
        scores(2 * i + 1, buf_b, cols_all, False)
        softmax_pv(2 * i, buf_a, cols_all)
        scores(2 * i + 2, buf_a, cols_all, False)
        softmax_pv(2 * i + 1, buf_b, cols_all)
        return carry

    lax.fori_loop(0, jnp.maximum(g - 1, 0), pair, 0)

    @pl.when(g > 0)
    def _():
        scores(2 * g - 1, buf_b, cols_all, False)
        softmax_pv(2 * g - 2, buf_a, cols_all)
        diag_a_block(buf_a)
        softmax_pv(2 * g - 1, buf_b, cols_all)

    scores(2 * g + 1, buf_b, cols_b, True)
    softmax_pv(2 * g, buf_a, cols_all)
    softmax_pv(2 * g + 1, buf_b, cols_b)

    lq = lq_ref[...]
    lam = (jnp.exp(jnp.sum(lq[0:1] * lq[1:2], axis=1, keepdims=True))
           - jnp.exp(jnp.sum(lq[2:3] * lq[3:4], axis=1, keepdims=True)) + lam_init)
    o1 = acc_ref[0, 0:V_DIM, :] / acc_ref[0, V_DIM:V_DIM + 1, :]
    o2 = acc_ref[1, 0:V_DIM, :] / acc_ref[1, V_DIM:V_DIM + 1, :]
    out_t = o1 - lam * o2
    out_t = out_t * lax.rsqrt(jnp.mean(out_t * out_t, axis=0, keepdims=True) + RMS_EPS)
    out = out_t.T * g_ref[...] * (1.0 - lam_init)
    o_ref[...] = out.astype(o_ref.dtype)


def _diff_attn(proj, lam_qk, subln_g, *, bsz, seq, heads, lam_init):
    n = proj.shape[0]
    t = min(512, seq // 2)
    tq = 2 * t
    nq = seq // tq
    kern = functools.partial(_attn_kernel, lam_init=lam_init)
    return pl.pallas_call(
        kern,
        grid=(bsz, heads, nq),
        in_specs=[
            pl.BlockSpec((4, QK_DIM), lambda b, h, qi: (0, 0)),
            pl.BlockSpec((1, V_DIM), lambda b, h, qi: (0, 0)),
            pl.BlockSpec((tq, HEAD_COLS), lambda b, h, qi: (b * nq + qi, h)),
            pl.BlockSpec((seq, HEAD_COLS), lambda b, h, qi: (b, heads + h)),
            pl.BlockSpec((seq, V_DIM), lambda b, h, qi: (b, 2 * heads + h)),
        ],
        out_specs=pl.BlockSpec((tq, V_DIM), lambda b, h, qi: (b * nq + qi, h)),
        out_shape=jax.ShapeDtypeStruct((n, heads * V_DIM), BF16),
        scratch_shapes=[
            pltpu.VMEM((V_DIM + ONES_ROWS, seq), BF16),
            pltpu.VMEM((2, HEAD_COLS, tq), BF16),
            pltpu.VMEM((2, t, tq), F32), pltpu.VMEM((2, t, tq), F32),
            pltpu.VMEM((2, tq), F32), pltpu.VMEM((2, tq), F32),
            pltpu.VMEM((2, tq), F32), pltpu.VMEM((2, V_DIM + ONES_ROWS, tq), F32),
        ],
        compiler_params=_params("parallel", "parallel", "arbitrary"),
        name="diff_attn",
    )(lam_qk, subln_g.reshape(1, V_DIM), proj, proj, proj)


def _cmul(ar, ai, br, bi):
    return ar * br - ai * bi, ar * bi + ai * br


def _ssm_prep_kernel(are_ref, aim_ref, ldt_ref, bre_ref, bim_ref,
                     bbre_ref, bbim_ref, hre_ref, him_ref, *, log2_sub):
    a_re = are_ref[...]
    a_im = aim_ref[...]
    dt = jnp.exp(ldt_ref[...])
    mag = jnp.exp(dt * a_re)
    ab_re = mag * jnp.cos(dt * a_im)
    ab_im = mag * jnp.sin(dt * a_im)
    den = a_re * a_re + a_im * a_im
    nr, ni = ab_re - 1.0, ab_im
    z_re = (nr * a_re + ni * a_im) / den
    z_im = (ni * a_re - nr * a_im) / den
    b_re = bre_ref[...]
    b_im = bim_ref[...]
    bbre_ref[...] = z_re * b_re - z_im * b_im
    bbim_ref[...] = z_re * b_im + z_im * b_re
    hre_ref[0:1, :] = ab_re
    him_ref[0:1, :] = ab_im
    er, ei = ab_re, ab_im
    for _ in range(log2_sub):
        er, ei = _cmul(er, ei, er, ei)
    for r in range(1, 4):
        hre_ref[r:r + 1, :] = er
        him_ref[r:r + 1, :] = ei
        er, ei = _cmul(er, ei, er, ei)


def _ssm_prep(a_re, a_im, log_dt, b_re, b_im, *, sub):
    g, p = a_re.shape
    w = g * p
    log2_sub = sub.bit_length() - 1
    assert sub == 1 << log2_sub
    row = lambda t: t.reshape(1, w)
    ldt = jnp.broadcast_to(log_dt[:, None], (g, p))
    tb = lambda t: jnp.transpose(t, (2, 0, 1)).reshape(SSM_GROUP, w)
    full = lambda r: pl.BlockSpec((r, w), lambda: (0, 0))
    shapes = [(SSM_GROUP, w), (SSM_GROUP, w), (4, w), (4, w)]
    return pl.pallas_call(
        functools.partial(_ssm_prep_kernel, log2_sub=log2_sub),
        in_specs=[full(1), full(1), full(1), full(SSM_GROUP), full(SSM_GROUP)],
        out_specs=[full(s[0]) for s in shapes],
        out_shape=[jax.ShapeDtypeStruct(s, F32) for s in shapes],
        compiler_params=pltpu.CompilerParams(vmem_limit_bytes=VMEM_LIMIT),
        name="ssm_prep",
    )(row(a_re), row(a_im), row(ldt), tb(b_re), tb(b_im))


def _block_diag_tiles(blocks, per_tile):
    g, r, c = blocks.shape
    t = blocks.reshape(g // per_tile, per_tile, r, c)
    eye = jnp.eye(per_tile, dtype=blocks.dtype)
    out = t[:, :, :, None, :] * eye[None, :, None, :, None]
    return out.reshape(g // per_tile, per_tile * r, per_tile * c)


GROUPS_PER_TILE = LANES // SSM_GROUP
STATES_PER_TILE = GROUPS_PER_TILE * SSM_STATE


def _ssm_kernel(u_ref, perm_ref, permt_ref, bb_ref, cre_ref, cim_ref, d_ref, wglu_ref,
                hre_ref, him_ref, o_ref, xr_ref, xi_ref, xb_ref, hr_ref, hi_ref):
    ci = pl.program_id(1)
    chunk = u_ref.shape[0]
    sub = chunk // SUBLANES
    width = xr_ref.shape[1]
    n_tiles = width // STATES_PER_TILE
    bcast = lambda row: jnp.broadcast_to(row, (SUBLANES, STATES_PER_TILE))

    @pl.when(ci == 0)
    def _():
        hr_ref[...] = jnp.zeros(hr_ref.shape, F32)
        hi_ref[...] = jnp.zeros(hi_ref.shape, F32)

    u = jnp.dot(perm_ref[...], u_ref[...], preferred_element_type=F32)
    ub = u.astype(BF16)
    row8 = lax.broadcasted_iota(jnp.int32, (SUBLANES, STATES_PER_TILE), 0)

    def drive(t):
        bu = jnp.dot(ub[:, t * LANES:(t + 1) * LANES], bb_ref[t], preferred_element_type=F32)
        sl = slice(t * STATES_PER_TILE, (t + 1) * STATES_PER_TILE)
        xr_ref[:, sl] = bu[:, :STATES_PER_TILE]
        xi_ref[:, sl] = bu[:, STATES_PER_TILE:]

    def scan(t):
        sl = slice(t * STATES_PER_TILE, (t + 1) * STATES_PER_TILE)
        sli = slice(width + sl.start, width + sl.stop)
        ar = bcast(hre_ref[0:1, sl])
        ai = bcast(him_ref[0:1, sl])
        xr = jnp.where(row8 == 0, bcast(hr_ref[:, sl]), 0.0)
        xi = jnp.where(row8 == 0, bcast(hi_ref[:, sl]), 0.0)
        for j in range(sub):
            rows = slice(j * SUBLANES, (j + 1) * SUBLANES)
            xr, xi = (ar * xr - ai * xi + xr_ref[rows, sl], ar * xi + ai * xr + xi_ref[rows, sl])
            xr_ref[rows, sl] = xr
            xi_ref[rows, sl] = xi
        er, ei = xr, xi
        for r, d in enumerate((1, 2, 4), start=1):
            pr = bcast(hre_ref[r:r + 1, sl])
            pi = bcast(him_ref[r:r + 1, sl])
            sr = jnp.where(row8 >= d, pltpu.roll(er, d, 0), 0.0)
            si = jnp.where(row8 >= d, pltpu.roll(ei, d, 0), 0.0)
            er, ei = er + pr * sr - pi * si, ei + pr * si + pi * sr
        hr_ref[:, sl] = er[SUBLANES - 1:SUBLANES, :]
        hi_ref[:, sl] = ei[SUBLANES - 1:SUBLANES, :]
        cr = jnp.where(row8 >= 1, pltpu.roll(er, 1, 0), 0.0)
        cim = jnp.where(row8 >= 1, pltpu.roll(ei, 1, 0), 0.0)
        for j in range(0, sub, 2):
            halves_r, halves_i = [], []
            for jj in (j, j + 1):
                rows = slice(jj * SUBLANES, (jj + 1) * SUBLANES)
                cr, cim = ar * cr - ai * cim, ar * cim + ai * cr
                halves_r.append(xr_ref[rows, sl] + cr)
                halves_i.append(xi_ref[rows, sl] + cim)
            rows2 = slice(j * SUBLANES, (j + 2) * SUBLANES)
            xb_ref[rows2, sl] = jnp.concatenate(halves_r, axis=0).astype(BF16)
            xb_ref[rows2, sli] = jnp.concatenate(halves_i, axis=0).astype(BF16)

    def readout(t):
        sl = slice(t * STATES_PER_TILE, (t + 1) * STATES_PER_TILE)
        sli = slice(width + sl.start, width + sl.stop)
        return (jnp.dot(xb_ref[:, sl], cre_ref[t], preferred_element_type=F32)
                - jnp.dot(xb_ref[:, sli], cim_ref[t], preferred_element_type=F32))

    ys = []
    drive(0)
    for t in range(n_tiles):
        if t + 1 < n_tiles:
            drive(t + 1)
        scan(t)
        ys.append(readout(t))
    y = jnp.concatenate(ys, axis=1) + d_ref[...] * u
    g = jax.nn.gelu(y)
    z = jnp.dot(g.astype(BF16), wglu_ref[...], preferred_element_type=F32)
    out = (g * jax.nn.sigmoid(z)).astype(BF16)
    o_ref[...] = jnp.dot(permt_ref[...], out, preferred_element_type=F32).astype(o_ref.dtype)


def _s5_ssm(proj, u_col_block, prep, c_re, c_im, d_skip, w_glu_bf, layer, *, bsz, seq, chunk):
    bb_re, bb_im, hs_re, hs_im = prep
    n = proj.shape[0]
    groups = c_re.shape[0]
    ssm_w = groups * SSM_GROUP
    width = groups * SSM_STATE
    n_tiles = ssm_w // LANES
    sub = chunk // SUBLANES
    nc = seq // chunk
    to_blocks = lambda t: jnp.transpose(t.reshape(SSM_GROUP, groups, SSM_STATE), (1, 0, 2))
    bb = jnp.concatenate([_block_diag_tiles(to_blocks(bb_re), GROUPS_PER_TILE),
                          _block_diag_tiles(to_blocks(bb_im), GROUPS_PER_TILE)], axis=2).astype(BF16)
    ct = lambda t: _block_diag_tiles(jnp.transpose(t, (0, 2, 1)), GROUPS_PER_TILE).astype(BF16)
    r = jnp.arange(chunk)
    src = (r % SUBLANES) * sub + r // SUBLANES
    perm = (src[:, None] == jnp.arange(chunk)[None, :]).astype(BF16)
    const2 = lambda shape: pl.BlockSpec(shape, lambda b, c: (0, 0))
    const3 = lambda shape: pl.BlockSpec(shape, lambda b, c: (0, 0, 0))
    return pl.pallas_call(
        _ssm_kernel,
        grid=(bsz, nc),
        in_specs=[
            pl.BlockSpec((chunk, ssm_w), lambda b, c: (b * nc + c, u_col_block)),
            const2((chunk, chunk)), const2((chunk, chunk)),
            const3((n_tiles, LANES, 2 * STATES_PER_TILE)),
            const3((n_tiles, STATES_PER_TILE, LANES)), const3((n_tiles, STATES_PER_TILE, LANES)),
            const2((1, ssm_w)), pl.BlockSpec((None, ssm_w, ssm_w), lambda b, c: (layer, 0, 0)),
            const2((4, width)), const2((4, width)),
        ],
        out_specs=pl.BlockSpec((chunk, ssm_w), lambda b, c: (b * nc + c, 0)),
        out_shape=jax.ShapeDtypeStruct((n, ssm_w), BF16),
        scratch_shapes=[
            pltpu.VMEM((chunk, width), F32), pltpu.VMEM((chunk, width), F32),
            pltpu.VMEM((chunk, 2 * width), BF16),
            pltpu.VMEM((1, width), F32), pltpu.VMEM((1, width), F32),
        ],
        compiler_params=_params("parallel", "arbitrary"),
        name="s5_ssm",
    )(proj, perm, perm.T, bb, ct(c_re), ct(c_im), d_skip.reshape(1, ssm_w), w_glu_bf, hs_re, hs_im)


def _layer_norm(y, g, b):
    mu = jnp.mean(y, axis=1, keepdims=True)
    yc = y - mu
    var = jnp.mean(yc * yc, axis=1, keepdims=True)
    return yc * lax.rsqrt(var + LN_EPS) * g + b


MIX_COLS = 512


def _mix_kernel(x_ref, attn_ref, ssm_ref, ga_ref, gs_ref, wa_ref, ws_ref, wo_ref, g_ref, b_ref,
                o_ref, mg_ref, *, alpha):
    d = o_ref.shape[1]
    chunks = [slice(c0, c0 + MIX_COLS) for c0 in range(0, d, MIX_COLS)]
    attn = attn_ref[...]
    ssm = ssm_ref[...]
    for cs in chunks:
        a = jnp.dot(attn, wa_ref[:, cs], preferred_element_type=F32)
        s = jnp.dot(ssm, ws_ref[:, cs], preferred_element_type=F32)
        mg_ref[:, cs] = (ga_ref[:, cs].astype(F32) * a + gs_ref[:, cs].astype(F32) * s).astype(BF16)
    merged = mg_ref[...]
    total = jnp.zeros((o_ref.shape[0], 1), F32)
    for cs in chunks:
        y = alpha * x_ref[:, cs] + jnp.dot(merged, wo_ref[:, cs], preferred_element_type=F32)
        o_ref[:, cs] = y
        total = total + jnp.sum(y, axis=1, keepdims=True)
    mu = total / d
    sq = jnp.zeros_like(total)
    for cs in chunks:
        yc = o_ref[:, cs] - mu
        sq = sq + jnp.sum(yc * yc, axis=1, keepdims=True)
    inv = lax.rsqrt(sq / d + LN_EPS)
    for cs in chunks:
        o_ref[:, cs] = (o_ref[:, cs] - mu) * inv * g_ref[:, cs] + b_ref[:, cs]


def _mix_out(x2, attn, ssm, proj, ga_block, wa, ws, wo, layer, ln_g, ln_b, *, alpha):
    n, d = x2.shape
    aw = attn.shape[1]
    sw = ssm.shape[1]
    tm = min(256, n)
    row = lambda w: pl.BlockSpec((tm, w), lambda i: (i, 0))
    const = lambda shape: pl.BlockSpec(shape, lambda i: (0, 0), pipeline_mode=pl.Buffered(1))
    wconst = lambda shape: pl.BlockSpec((None,) + shape, lambda i: (layer, 0, 0),
                                        pipeline_mode=pl.Buffered(1))
    return pl.pallas_call(
        functools.partial(_mix_kernel, alpha=alpha),
        grid=(n // tm,),
        in_specs=[
            row(d), row(aw), row(sw),
            pl.BlockSpec((tm, d), lambda i: (i, ga_block)),
            pl.BlockSpec((tm, d), lambda i: (i, ga_block + 1)),
            wconst((aw, d)), wconst((sw, d)), wconst((d, d)), const((1, d)), const((1, d)),
        ],
        out_specs=row(d),
        out_shape=jax.ShapeDtypeStruct((n, d), F32),
        scratch_shapes=[pltpu.VMEM((tm, d), BF16)],
        compiler_params=_params("parallel"),
        name="mix_out",
    )(x2, attn, ssm, proj, proj, wa, ws, wo, ln_g.reshape(1, d), ln_b.reshape(1, d))


def _mlp_kernel(x_ref, w1_ref, w2_ref, g_ref, b_ref, o_ref, xb_ref, acc_ref, *, alpha):
    f = pl.program_id(1)

    @pl.when(f == 0)
    def _():
        xb_ref[...] = x_ref[...].astype(BF16)
        acc_ref[...] = jnp.zeros(acc_ref.shape, F32)

    h = jnp.dot(xb_ref[...], w1_ref[...], preferred_element_type=F32)
    h = jnp.square(jnp.maximum(h, 0.0))
    acc_ref[...] += jnp.dot(h.astype(BF16), w2_ref[...], preferred_element_type=F32)

    @pl.when(f == pl.num_programs(1) - 1)
    def _():
        o_ref[...] = _layer_norm(alpha * x_ref[...] + acc_ref[...], g_ref[...], b_ref[...])


def _mlp(x2, w1, w2, layer, ln_g, ln_b, *, alpha):
    n, d = x2.shape
    dff = w1.shape[2]
    tm = min(512, n)
    tf = min(1024, dff)
    return pl.pallas_call(
        functools.partial(_mlp_kernel, alpha=alpha),
        grid=(n // tm, dff // tf),
        in_specs=[
            pl.BlockSpec((tm, d), lambda i, f: (i, 0)),
            pl.BlockSpec((None, d, tf), lambda i, f: (layer, 0, f)),
            pl.BlockSpec((None, tf, d), lambda i, f: (layer, f, 0)),
            pl.BlockSpec((1, d), lambda i, f: (0, 0)),
            pl.BlockSpec((1, d), lambda i, f: (0, 0)),
        ],
        out_specs=pl.BlockSpec((tm, d), lambda i, f: (i, 0)),
        out_shape=jax.ShapeDtypeStruct((n, d), F32),
        scratch_shapes=[pltpu.VMEM((tm, d), BF16), pltpu.VMEM((tm, d), F32)],
        compiler_params=_params("parallel", "arbitrary"),
        name="mlp",
    )(x2, w1, w2, ln_g.reshape(1, d), ln_b.reshape(1, d))


def _rope_tables(positions):
    inv = ROPE_THETA ** (-jnp.arange(0, QK_DIM, 2, dtype=F32) / QK_DIM)
    ang = positions.astype(F32).reshape(-1, 1) * inv
    cos = jnp.cos(ang)
    sin = jnp.sin(ang)
    reps = LANES // QK_DIM
    return (jnp.tile(jnp.concatenate([cos, cos], axis=1), (1, reps)),
            jnp.tile(jnp.concatenate([-sin, sin], axis=1), (1, reps)))


def kernel(x, positions, w_in, lambda_qk, subln_g, ssm_a_re, ssm_a_im, ssm_log_dt, ssm_b_re,
           ssm_b_im, ssm_c_re, ssm_c_im, ssm_d, w_glu, w_attn_up, w_ssm_up, w_out, ln1_g, ln1_b,
           ln2_g, ln2_b, w_mlp_up, w_mlp_down):
    bsz, seq, d = x.shape
    depth = w_in.shape[0]
    attn_w = w_attn_up.shape[1]
    ssm_w = w_ssm_up.shape[1]
    heads = attn_w // V_DIM
    q_cols = heads * HEAD_COLS
    assert w_in.shape[2] == 2 * q_cols + attn_w + ssm_w + 2 * d
    assert q_cols == attn_w == ssm_w and d % ssm_w == 0
    alpha = (2.0 * depth) ** 0.25
    chunk = min(512, seq)

    cos_t, sin_t = _rope_tables(positions)
    w_in_bf, w_glu_bf, w_attn_up_bf, w_ssm_up_bf, w_out_bf, w_mlp_up_bf, w_mlp_down_bf = (
        w.astype(BF16) for w in (w_in, w_glu, w_attn_up, w_ssm_up, w_out, w_mlp_up, w_mlp_down))
    x2 = x.reshape(bsz * seq, d)
    for l in range(depth):
        lam_init = _lambda_init(l)
        proj = _in_proj(x2, w_in_bf, l, cos_t, sin_t, q_cols=q_cols,
                        rope_cols=2 * q_cols, plain_cols=attn_w + ssm_w)
        attn = _diff_attn(proj, lambda_qk[l], subln_g[l], bsz=bsz, seq=seq, heads=heads,
                          lam_init=lam_init)
        prep = _ssm_prep(ssm_a_re[l], ssm_a_im[l], ssm_log_dt[l], ssm_b_re[l], ssm_b_im[l],
                         sub=chunk // SUBLANES)
        ssm = _s5_ssm(proj, (2 * q_cols + attn_w) // ssm_w, prep, ssm_c_re[l], ssm_c_im[l],
                      ssm_d[l], w_glu_bf, l, bsz=bsz, seq=seq, chunk=chunk)
        x2 = _mix_out(x2, attn, ssm, proj, (2 * q_cols + attn_w + ssm_w) // d,
                      w_attn_up_bf, w_ssm_up_bf, w_out_bf, l, ln1_g[l], ln1_b[l], alpha=alpha)
        x2 = _mlp(x2, w_mlp_up_bf, w_mlp_down_bf, l, ln2_g[l], ln2_b[l], alpha=alpha)
    return x2.reshape(bsz, seq, d)
```

```python
import functools
import math

import jax
import jax.numpy as jnp
from jax import lax
from jax.experimental import pallas as pl
from jax.experimental.pallas import tpu as pltpu

F32 = jnp.float32
BF16 = jnp.bfloat16

QK_DIM = 64
V_DIM = 2 * QK_DIM
HEAD_COLS = 2 * QK_DIM
SSM_GROUP = 16
SSM_STATE = 64
ROPE_THETA = 10000.0
LN_EPS = 1e-5
RMS_EPS = 1e-5
LANES = 128
SUBLANES = 8
VMEM_LIMIT = 56 * 1024 * 1024
NEG_BIG = -1e30


def _lambda_init(layer):
    return 0.8 - 0.6 * math.exp(-0.3 * layer)


def _params(*sem):
    return pltpu.CompilerParams(dimension_semantics=sem, vmem_limit_bytes=VMEM_LIMIT)


EPILOGUE_COLS = 256


def _in_proj_kernel(x_ref, w_ref, cos_ref, sin_ref, o_ref, xb_ref, *, q_cols, rope_tiles,
                    plain_tiles):
    j = pl.program_id(1)
    tn = o_ref.shape[1]
    chunks = [slice(c0, c0 + EPILOGUE_COLS) for c0 in range(0, tn, EPILOGUE_COLS)]

    @pl.when(j == 0)
    def _():
        xb_ref[...] = x_ref[...].astype(BF16)

    def matmul(cs):
        return jnp.dot(xb_ref[...], w_ref[:, cs], preferred_element_type=F32)

    @pl.when(j < rope_tiles)
    def _():
        reps = EPILOGUE_COLS // LANES
        lane = lax.broadcasted_iota(jnp.int32, (o_ref.shape[0], EPILOGUE_COLS), 1)
        first_half = (lane % QK_DIM) < (QK_DIM // 2)
        for cs in chunks:
            scale = jnp.where(j * tn + cs.start < q_cols, QK_DIM ** -0.5 * math.log2(math.e),
                              1.0).astype(F32)
            c = jnp.tile(cos_ref[...] * scale, (1, reps))
            s = jnp.tile(sin_ref[...] * scale, (1, reps))
            acc = matmul(cs)
            rot = jnp.where(first_half, pltpu.roll(acc, EPILOGUE_COLS - QK_DIM // 2, 1),
                            pltpu.roll(acc, QK_DIM // 2, 1))
            o_ref[:, cs] = (acc * c + rot * s).astype(o_ref.dtype)

    @pl.when((j >= rope_tiles) & (j < rope_tiles + plain_tiles))
    def _():
        for cs in chunks:
            o_ref[:, cs] = matmul(cs).astype(o_ref.dtype)

    @pl.when(j >= rope_tiles + plain_tiles)
    def _():
        for cs in chunks:
            o_ref[:, cs] = jax.nn.sigmoid(matmul(cs)).astype(o_ref.dtype)


def _in_proj(x2, w_bf, layer, cos_t, sin_t, *, q_cols, rope_cols, plain_cols):
    n, d = x2.shape
    cols = w_bf.shape[2]
    tm = min(1024, n)
    tn = min(2048, 2 * q_cols)
    kern = functools.partial(_in_proj_kernel, q_cols=q_cols, rope_tiles=rope_cols // tn,
                             plain_tiles=plain_cols // tn)
    return pl.pallas_call(
        kern,
        grid=(n // tm, cols // tn),
        in_specs=[
            pl.BlockSpec((tm, d), lambda i, j: (i, 0)),
            pl.BlockSpec((None, d, tn), lambda i, j: (layer, 0, j)),
            pl.BlockSpec((tm, LANES), lambda i, j: (i, 0)),
            pl.BlockSpec((tm, LANES), lambda i, j: (i, 0)),
        ],
        out_specs=pl.BlockSpec((tm, tn), lambda i, j: (i, j)),
        out_shape=jax.ShapeDtypeStruct((n, cols), BF16),
        scratch_shapes=[pltpu.VMEM((tm, d), BF16)],
        compiler_params=_params("parallel", "arbitrary"),
        name="in_proj",
    )(x2, w_bf, cos_t, sin_t)


ONES_ROWS = 16


def _attn_kernel(lq_ref, g_ref, q_ref, k_ref, v_ref, o_ref, vt_ref, qt_ref, sa_ref, sb_ref, bma_ref, bmb_ref,
                 m_ref, acc_ref, *, lam_init):
    g = pl.program_id(2)
    t = q_ref.shape[0] // 2
    seq = k_ref.shape[0]
    cols_all, cols_a, cols_b = slice(0, 2 * t), slice(0, t), slice(t, 2 * t)

    @pl.when(g == 0)
    def _():
        for r in range(seq // t):
            rows = slice(r * t, (r + 1) * t)
            vt_ref[0:V_DIM, rows] = v_ref[rows, :].astype(F32).T.astype(BF16)
        vt_ref[V_DIM:V_DIM + ONES_ROWS, :] = jnp.ones((ONES_ROWS, seq), BF16)

    qt = q_ref[...].astype(F32).T
    comp = lax.broadcasted_iota(jnp.int32, qt.shape, 0) // QK_DIM
    for c in range(2):
        qt_ref[c] = jnp.where(comp == c, qt, 0.0).astype(BF16)
    m_ref[...] = jnp.full(m_ref.shape, NEG_BIG, F32)
    acc_ref[...] = jnp.zeros(acc_ref.shape, F32)

    def scores(ki, buf, cols, masked):
        s_ref, bm_ref = buf
        k = k_ref[pl.ds(pl.multiple_of(ki * t, t), t), :]
        if masked:
            kpos = lax.broadcasted_iota(jnp.int32, (t, t), 0)
            qpos = lax.broadcasted_iota(jnp.int32, (t, t), 1)
            keep = kpos <= qpos
        for c in range(2):
            st = jnp.dot(k, qt_ref[c, :, cols], preferred_element_type=F32)
            if masked:
                st = jnp.where(keep, st, NEG_BIG)
            s_ref[c, :, cols] = st
            bm_ref[c:c + 1, cols] = jnp.max(st, axis=0, keepdims=True)

    def softmax_pv(ki, buf, cols):
        s_ref, bm_ref = buf
        vt = vt_ref[:, pl.ds(pl.multiple_of(ki * t, t), t)]
        for c in range(2):
            m_prev = m_ref[c:c + 1, cols]
            m_new = jnp.maximum(m_prev, bm_ref[c:c + 1, cols])
            alpha = jnp.exp2(m_prev - m_new)
            p = jnp.exp2(s_ref[c, :, cols] - m_new).astype(BF16)
            acc_ref[c, :, cols] = (alpha * acc_ref[c, :, cols]
                                   + jnp.dot(vt, p, preferred_element_type=F32))
            m_ref[c:c + 1, cols] = m_new

    buf_a = (sa_ref, bma_ref)
    buf_b = (sb_ref, bmb_ref)

    def diag_a_block(buf):
        scores(2 * g, buf, cols_a, True)
        scores(2 * g, buf, cols_b, False)

    @pl.when(g == 0)
    def _():
        diag_a_block(buf_a)

    @pl.when(g > 0)
    def _():
        scores(0, buf_a, cols_all, False)

    def pair(i, carry):
        scores(2 * i + 1, buf_b, cols_all, False)
        softmax_pv(2 * i, buf_a, cols_all)
        scores(2 * i + 2, buf_a, cols_all, False)
        softmax_pv(2 * i + 1, buf_b, cols_all)
        return carry

    lax.fori_loop(0, jnp.maximum(g - 1, 0), pair, 0)

    @pl.when(g > 0)
    def _():
        scores(2 * g - 1, buf_b, cols_all, False)
        softmax_pv(2 * g - 2, buf_a, cols_all)
        diag_a_block(buf_a)
        softmax_pv(2 * g - 1, buf_b, cols_all)

    scores(2 * g + 1, buf_b, cols_b, True)
    softmax_pv(2 * g, buf_a, cols_all)
    softmax_pv(2 * g + 1, buf_b, cols_b)

    lq = lq_ref[...]
    lam = (jnp.exp(jnp.sum(lq[0:1] * lq[1:2], axis=1, keepdims=True))
           - jnp.exp(jnp.sum(lq[2:3] * lq[3:4], axis=1, keepdims=True)) + lam_init)
    o1 = acc_ref[0, 0:V_DIM, :] / acc_ref[0, V_DIM:V_DIM + 1, :]
    o2 = acc_ref[1, 0:V_DIM, :] / acc_ref[1, V_DIM:V_DIM + 1, :]
    out_t = o1 - lam * o2
    out_t = out_t * lax.rsqrt(jnp.mean(out_t * out_t, axis=0, keepdims=True) + RMS_EPS)
    out = out_t.T * g_ref[...] * (1.0 - lam_init)
    o_ref[...] = out.astype(o_ref.dtype)


def _diff_attn(proj, lam_qk, subln_g, *, bsz, seq, heads, lam_init):
    n = proj.shape[0]
    t = min(512, seq // 2)
    tq = 2 * t
    nq = seq // tq
    kern = functools.partial(_attn_kernel, lam_init=lam_init)
    return pl.pallas_call(
        kern,
        grid=(bsz, heads, nq),
        in_specs=[
            pl.BlockSpec((4, QK_DIM), lambda b, h, qi: (0, 0)),
            pl.BlockSpec((1, V_DIM), lambda b, h, qi: (0, 0)),
            pl.BlockSpec((tq, HEAD_COLS), lambda b, h, qi: (b * nq + qi, h)),
            pl.BlockSpec((seq, HEAD_COLS), lambda b, h, qi: (b, heads + h)),
            pl.BlockSpec((seq, V_DIM), lambda b, h, qi: (b, 2 * heads + h)),
        ],
        out_specs=pl.BlockSpec((tq, V_DIM), lambda b, h, qi: (b * nq + qi, h)),
        out_shape=jax.ShapeDtypeStruct((n, heads * V_DIM), BF16),
        scratch_shapes=[
            pltpu.VMEM((V_DIM + ONES_ROWS, seq), BF16),
            pltpu.VMEM((2, HEAD_COLS, tq), BF16),
            pltpu.VMEM((2, t, tq), F32), pltpu.VMEM((2, t, tq), F32),
            pltpu.VMEM((2, tq), F32), pltpu.VMEM((2, tq), F32),
            pltpu.VMEM((2, tq), F32), pltpu.VMEM((2, V_DIM + ONES_ROWS, tq), F32),
        ],
        compiler_params=_params("parallel", "parallel", "arbitrary"),
        name="diff_attn",
    )(lam_qk, subln_g.reshape(1, V_DIM), proj, proj, proj)


def _cmul(ar, ai, br, bi):
    return ar * br - ai * bi, ar * bi + ai * br


def _ssm_prep_kernel(are_ref, aim_ref, ldt_ref, bre_ref, bim_ref,
                     bbre_ref, bbim_ref, hre_ref, him_ref, *, log2_sub):
    a_re = are_ref[...]
    a_im = aim_ref[...]
    dt = jnp.exp(ldt_ref[...])
    mag = jnp.exp(dt * a_re)
    ab_re = mag * jnp.cos(dt * a_im)
    ab_im = mag * jnp.sin(dt * a_im)
    den = a_re * a_re + a_im * a_im
    nr, ni = ab_re - 1.0, ab_im
    z_re = (nr * a_re + ni * a_im) / den
    z_im = (ni * a_re - nr * a_im) / den
    b_re = bre_ref[...]
    b_im = bim_ref[...]
    bbre_ref[...] = z_re * b_re - z_im * b_im
    bbim_ref[...] = z_re * b_im + z_im * b_re
    hre_ref[0:1, :] = ab_re
    him_ref[0:1, :] = ab_im
    er, ei = ab_re, ab_im
    for _ in range(log2_sub):
        er, ei = _cmul(er, ei, er, ei)
    for r in range(1, 4):
        hre_ref[r:r + 1, :] = er
        him_ref[r:r + 1, :] = ei
        er, ei = _cmul(er, ei, er, ei)


def _ssm_prep(a_re, a_im, log_dt, b_re, b_im, *, sub):
    g, p = a_re.shape
    w = g * p
    log2_sub = sub.bit_length() - 1
    assert sub == 1 << log2_sub
    row = lambda t: t.reshape(1, w)
    ldt = jnp.broadcast_to(log_dt[:, None], (g, p))
    tb = lambda t: jnp.transpose(t, (2, 0, 1)).reshape(SSM_GROUP, w)
    full = lambda r: pl.BlockSpec((r, w), lambda: (0, 0))
    shapes = [(SSM_GROUP, w), (SSM_GROUP, w), (4, w), (4, w)]
    return pl.pallas_call(
        functools.partial(_ssm_prep_kernel, log2_sub=log2_sub),
        in_specs=[full(1), full(1), full(1), full(SSM_GROUP), full(SSM_GROUP)],
        out_specs=[full(s[0]) for s in shapes],
        out_shape=[jax.ShapeDtypeStruct(s, F32) for s in shapes],
        compiler_params=pltpu.CompilerParams(vmem_limit_bytes=VMEM_LIMIT),
        name="ssm_prep",
    )(row(a_re), row(a_im), row(ldt), tb(b_re), tb(b_im))


def _block_diag_tiles(blocks, per_tile):
    g, r, c = blocks.shape
    t = blocks.reshape(g // per_tile, per_tile, r, c)
    eye = jnp.eye(per_tile, dtype=blocks.dtype)
    out = t[:, :, :, None, :] * eye[None, :, None, :, None]
    return out.reshape(g // per_tile, per_tile * r, per_tile * c)


GROUPS_PER_TILE = LANES // SSM_GROUP
STATES_PER_TILE = GROUPS_PER_TILE * SSM_STATE


def _ssm_kernel(u_ref, perm_ref, permt_ref, bb_ref, cre_ref, cim_ref, d_ref, wglu_ref,
                hre_ref, him_ref, o_ref, xr_ref, xi_ref, xb_ref, hr_ref, hi_ref):
    ci = pl.program_id(1)
    chunk = u_ref.shape[0]
    sub = chunk // SUBLANES
    width = xr_ref.shape[1]
    n_tiles = width // STATES_PER_TILE
    bcast = lambda row: jnp.broadcast_to(row, (SUBLANES, STATES_PER_TILE))

    @pl.when(ci == 0)
    def _():
        hr_ref[...] = jnp.zeros(hr_ref.shape, F32)
        hi_ref[...] = jnp.zeros(hi_ref.shape, F32)

    u = jnp.dot(perm_ref[...], u_ref[...], preferred_element_type=F32)
    ub = u.astype(BF16)
    row8 = lax.broadcasted_iota(jnp.int32, (SUBLANES, STATES_PER_TILE), 0)

    def drive(t):
        bu = jnp.dot(ub[:, t * LANES:(t + 1) * LANES], bb_ref[t], preferred_element_type=F32)
        sl = slice(t * STATES_PER_TILE, (t + 1) * STATES_PER_TILE)
        xr_ref[:, sl] = bu[:, :STATES_PER_TILE]
        xi_ref[:, sl] = bu[:, STATES_PER_TILE:]

    def scan(t):
        sl = slice(t * STATES_PER_TILE, (t + 1) * STATES_PER_TILE)
        sli = slice(width + sl.start, width + sl.stop)
        ar = bcast(hre_ref[0:1, sl])
        ai = bcast(him_ref[0:1, sl])
        xr = jnp.where(row8 == 0, bcast(hr_ref[:, sl]), 0.0)
        xi = jnp.where(row8 == 0, bcast(hi_ref[:, sl]), 0.0)
        for j in range(sub):
            rows = slice(j * SUBLANES, (j + 1) * SUBLANES)
            xr, xi = (ar * xr - ai * xi + xr_ref[rows, sl], ar * xi + ai * xr + xi_ref[rows, sl])
            xr_ref[rows, sl] = xr
            xi_ref[rows, sl] = xi
        er, ei = xr, xi
        for r, d in enumerate((1, 2, 4), start=1):
            pr = bcast(hre_ref[r:r + 1, sl])
            pi = bcast(him_ref[r:r + 1, sl])
            sr = jnp.where(row8 >= d, pltpu.roll(er, d, 0), 0.0)
            si = jnp.where(row8 >= d, pltpu.roll(ei, d, 0), 0.0)
            er, ei = er + pr * sr - pi * si, ei + pr * si + pi * sr
        hr_ref[:, sl] = er[SUBLANES - 1:SUBLANES, :]
        hi_ref[:, sl] = ei[SUBLANES - 1:SUBLANES, :]
        cr = jnp.where(row8 >= 1, pltpu.roll(er, 1, 0), 0.0)
        cim = jnp.where(row8 >= 1, pltpu.roll(ei, 1, 0), 0.0)
        for j in range(0, sub, 2):
            halves_r, halves_i = [], []
            for jj in (j, j + 1):
                rows = slice(jj * SUBLANES, (jj + 1) * SUBLANES)
                cr, cim = ar * cr - ai * cim, ar * cim + ai * cr
                halves_r.append(xr_ref[rows, sl] + cr)
                halves_i.append(xi_ref[rows, sl] + cim)
            rows2 = slice(j * SUBLANES, (j + 2) * SUBLANES)
            xb_ref[rows2, sl] = jnp.concatenate(halves_r, axis=0).astype(BF16)
            xb_ref[rows2, sli] = jnp.concatenate(halves_i, axis=0).astype(BF16)

    def readout(t):
        sl = slice(t * STATES_PER_TILE, (t + 1) * STATES_PER_TILE)
        sli = slice(width + sl.start, width + sl.stop)
        return (jnp.dot(xb_ref[:, sl], cre_ref[t], preferred_element_type=F32)
                - jnp.dot(xb_ref[:, sli], cim_ref[t], preferred_element_type=F32))

    ys = []
    drive(0)
    for t in range(n_tiles):
        if t + 1 < n_tiles:
            drive(t + 1)
        scan(t)
        ys.append(readout(t))
    y = jnp.concatenate(ys, axis=1) + d_ref[...] * u
    g = jax.nn.gelu(y)
    z = jnp.dot(g.astype(BF16), wglu_ref[...], preferred_element_type=F32)
    out = (g * jax.nn.sigmoid(z)).astype(BF16)
    o_ref[...] = jnp.dot(permt_ref[...], out, preferred_element_type=F32).astype(o_ref.dtype)


def _s5_ssm(proj, u_col_block, prep, c_re, c_im, d_skip, w_glu_bf, layer, *, bsz, seq, chunk):
    bb_re, bb_im, hs_re, hs_im = prep
    n = proj.shape[0]
    groups = c_re.shape[0]
    ssm_w = groups * SSM_GROUP
    width = groups * SSM_STATE
    n_tiles = ssm_w // LANES
    sub = chunk // SUBLANES
    nc = seq // chunk
    to_blocks = lambda t: jnp.transpose(t.reshape(SSM_GROUP, groups, SSM_STATE), (1, 0, 2))
    bb = jnp.concatenate([_block_diag_tiles(to_blocks(bb_re), GROUPS_PER_TILE),
                          _block_diag_tiles(to_blocks(bb_im), GROUPS_PER_TILE)], axis=2).astype(BF16)
    ct = lambda t: _block_diag_tiles(jnp.transpose(t, (0, 2, 1)), GROUPS_PER_TILE).astype(BF16)
    r = jnp.arange(chunk)
    src = (r % SUBLANES) * sub + r // SUBLANES
    perm = (src[:, None] == jnp.arange(chunk)[None, :]).astype(BF16)
    const2 = lambda shape: pl.BlockSpec(shape, lambda b, c: (0, 0))
    const3 = lambda shape: pl.BlockSpec(shape, lambda b, c: (0, 0, 0))
    return pl.pallas_call(
        _ssm_kernel,
        grid=(bsz, nc),
        in_specs=[
            pl.BlockSpec((chunk, ssm_w), lambda b, c: (b * nc + c, u_col_block)),
            const2((chunk, chunk)), const2((chunk, chunk)),
            const3((n_tiles, LANES, 2 * STATES_PER_TILE)),
            const3((n_tiles, STATES_PER_TILE, LANES)), const3((n_tiles, STATES_PER_TILE, LANES)),
            const2((1, ssm_w)), pl.BlockSpec((None, ssm_w, ssm_w), lambda b, c: (layer, 0, 0)),
            const2((4, width)), const2((4, width)),
        ],
        out_specs=pl.BlockSpec((chunk, ssm_w), lambda b, c: (b * nc + c, 0)),
        out_shape=jax.ShapeDtypeStruct((n, ssm_w), BF16),
        scratch_shapes=[
            pltpu.VMEM((chunk, width), F32), pltpu.VMEM((chunk, width), F32),
            pltpu.VMEM((chunk, 2 * width), BF16),
            pltpu.VMEM((1, width), F32), pltpu.VMEM((1, width), F32),
        ],
        compiler_params=_params("parallel", "arbitrary"),
        name="s5_ssm",
    )(proj, perm, perm.T, bb, ct(c_re), ct(c_im), d_skip.reshape(1, ssm_w), w_glu_bf, hs_re, hs_im)


def _layer_norm(y, g, b):
    mu = jnp.mean(y, axis=1, keepdims=True)
    yc = y - mu
    var = jnp.mean(yc * yc, axis=1, keepdims=True)
    return yc * lax.rsqrt(var + LN_EPS) * g + b


def _mix_kernel(x_ref, attn_ref, ssm_ref, ga_ref, gs_ref, wa_ref, ws_ref, wo_ref, g_ref, b_ref,
                o_ref, *, alpha):
    a = jnp.dot(attn_ref[...], wa_ref[...], preferred_element_type=F32)
    s = jnp.dot(ssm_ref[...], ws_ref[...], preferred_element_type=F32)
    merged = ga_ref[...].astype(F32) * a + gs_ref[...].astype(F32) * s
    mix = jnp.dot(merged.astype(BF16), wo_ref[...], preferred_element_type=F32)
    o_ref[...] = _layer_norm(alpha * x_ref[...] + mix, g_ref[...], b_ref[...])


def _mix_out(x2, attn, ssm, proj, ga_block, wa, ws, wo, layer, ln_g, ln_b, *, alpha):
    n, d = x2.shape
    aw = attn.shape[1]
    sw = ssm.shape[1]
    tm = min(256, n)
    row = lambda w: pl.BlockSpec((tm, w), lambda i: (i, 0))
    const = lambda shape: pl.BlockSpec(shape, lambda i: (0, 0), pipeline_mode=pl.Buffered(1))
    wconst = lambda shape: pl.BlockSpec((None,) + shape, lambda i: (layer, 0, 0),
                                        pipeline_mode=pl.Buffered(1))
    return pl.pallas_call(
        functools.partial(_mix_kernel, alpha=alpha),
        grid=(n // tm,),
        in_specs=[
            row(d), row(aw), row(sw),
            pl.BlockSpec((tm, d), lambda i: (i, ga_block)),
            pl.BlockSpec((tm, d), lambda i: (i, ga_block + 1)),
            wconst((aw, d)), wconst((sw, d)), wconst((d, d)), const((1, d)), const((1, d)),
        ],
        out_specs=row(d),
        out_shape=jax.ShapeDtypeStruct((n, d), F32),
        compiler_params=_params("parallel"),
        name="mix_out",
    )(x2, attn, ssm, proj, proj, wa, ws, wo, ln_g.reshape(1, d), ln_b.reshape(1, d))


def _mlp_kernel(x_ref, w1_ref, w2_ref, g_ref, b_ref, o_ref, xb_ref, acc_ref, *, alpha):
    f = pl.program_id(1)

    @pl.when(f == 0)
    def _():
        xb_ref[...] = x_ref[...].astype(BF16)
        acc_ref[...] = jnp.zeros(acc_ref.shape, F32)

    h = jnp.dot(xb_ref[...], w1_ref[...], preferred_element_type=F32)
    h = jnp.square(jnp.maximum(h, 0.0))
    acc_ref[...] += jnp.dot(h.astype(BF16), w2_ref[...], preferred_element_type=F32)

    @pl.when(f == pl.num_programs(1) - 1)
    def _():
        o_ref[...] = _layer_norm(alpha * x_ref[...] + acc_ref[...], g_ref[...], b_ref[...])


def _mlp(x2, w1, w2, layer, ln_g, ln_b, *, alpha):
    n, d = x2.shape
    dff = w1.shape[2]
    tm = min(512, n)
    tf = min(1024, dff)
    return pl.pallas_call(
        functools.partial(_mlp_kernel, alpha=alpha),
        grid=(n // tm, dff // tf),
        in_specs=[
            pl.BlockSpec((tm, d), lambda i, f: (i, 0)),
            pl.BlockSpec((None, d, tf), lambda i, f: (layer, 0, f)),
            pl.BlockSpec((None, tf, d), lambda i, f: (layer, f, 0)),
            pl.BlockSpec((1, d), lambda i, f: (0, 0)),
            pl.BlockSpec((1, d), lambda i, f: (0, 0)),
        ],
        out_specs=pl.BlockSpec((tm, d), lambda i, f: (i, 0)),
        out_shape=jax.ShapeDtypeStruct((n, d), F32),
        scratch_shapes=[pltpu.VMEM((tm, d), BF16), pltpu.VMEM((tm, d), F32)],
        compiler_params=_params("parallel", "arbitrary"),
        name="mlp",
    )(x2, w1, w2, ln_g.reshape(1, d), ln_b.reshape(1, d))


def _rope_tables(positions):
    inv = ROPE_THETA ** (-jnp.arange(0, QK_DIM, 2, dtype=F32) / QK_DIM)
    ang = positions.astype(F32).reshape(-1, 1) * inv
    cos = jnp.cos(ang)
    sin = jnp.sin(ang)
    reps = LANES // QK_DIM
    return (jnp.tile(jnp.concatenate([cos, cos], axis=1), (1, reps)),
            jnp.tile(jnp.concatenate([-sin, sin], axis=1), (1, reps)))


def kernel(x, positions, w_in, lambda_qk, subln_g, ssm_a_re, ssm_a_im, ssm_log_dt, ssm_b_re,
           ssm_b_im, ssm_c_re, ssm_c_im, ssm_d, w_glu, w_attn_up, w_ssm_up, w_out, ln1_g, ln1_b,
           ln2_g, ln2_b, w_mlp_up, w_mlp_down):
    bsz, seq, d = x.shape
    depth = w_in.shape[0]
    attn_w = w_attn_up.shape[1]
    ssm_w = w_ssm_up.shape[1]
    heads = attn_w // V_DIM
    q_cols = heads * HEAD_COLS
    assert w_in.shape[2] == 2 * q_cols + attn_w + ssm_w + 2 * d
    assert q_cols == attn_w == ssm_w and d % ssm_w == 0
    alpha = (2.0 * depth) ** 0.25
    chunk = min(512, seq)

    cos_t, sin_t = _rope_tables(positions)
    w_in_bf, w_glu_bf, w_attn_up_bf, w_ssm_up_bf, w_out_bf, w_mlp_up_bf, w_mlp_down_bf = (
        w.astype(BF16) for w in (w_in, w_glu, w_attn_up, w_ssm_up, w_out, w_mlp_up, w_mlp_down))
    x2 = x.reshape(bsz * seq, d)
    for l in range(depth):
        lam_init = _lambda_init(l)
        proj = _in_proj(x2, w_in_bf, l, cos_t, sin_t, q_cols=q_cols,
                        rope_cols=2 * q_cols, plain_cols=attn_w + ssm_w)
        attn = _diff_attn(proj, lambda_qk[l], subln_g[l], bsz=bsz, seq=seq, heads=heads,
                          lam_init=lam_init)
        prep = _ssm_prep(ssm_a_re[l], ssm_a_im[l], ssm_log_dt[l], ssm_b_re[l], ssm_b_im[l],
                         sub=chunk // SUBLANES)
        ssm = _s5_ssm(proj, (2 * q_cols + attn_w) // ssm_w, prep, ssm_c_re[l], ssm_c_im[l],
                      ssm_d[l], w_glu_bf, l, bsz=bsz, seq=seq, chunk=chunk)
        x2 = _mix_out(x2, attn, ssm, proj, (2 * q_cols + attn_w + ssm_w) // d,
                      w_attn_up_bf, w_ssm_up_bf, w_out_bf, l, ln1_g[l], ln1_b[l], alpha=alpha)
        x2 = _mlp(x2, w_mlp_up_bf, w_mlp_down_bf, l, ln2_g[l], ln2_b[l], alpha=alpha)
    return x2.reshape(bsz, seq, d)
```
